```python
import math, functools
import jax, jax.numpy as jnp
from jax import lax
import numpy as np

D_MODEL = 1024
BATCH = 8
SEQ = 4096
DEPTH = 2

GRID_W = 64
CTX_LEN = 256
HEAD_DIM = 64
ROPE_THETA = 10000.0
NORM_EPS = 1e-6
Q_BLOCK = 128

HY_WIDTH = D_MODEL // 4
HY_ORDER = 2
HY_EMB = 33
HY_BANDS = (HY_EMB - 1) // 2
HY_HIDDEN = 64
HY_FAST_DECAY = 0.3
HY_SLOW_DECAY = 1.5
HY_TARGET = 1e-2

GQA_HEADS = D_MODEL // 2 // HEAD_DIM
GQA_KV_HEADS = GQA_HEADS // 4
GQA_GROUP = GQA_HEADS // GQA_KV_HEADS

DIFF_HEADS = 4
DIFF_WIDTH = D_MODEL // 4
DIFF_QK_DIM = DIFF_WIDTH // (2 * DIFF_HEADS)
DIFF_V_DIM = 2 * DIFF_QK_DIM

MIX_WIDTH = HY_WIDTH + GQA_HEADS * HEAD_DIM + DIFF_HEADS * DIFF_V_DIM

HY_COLS = (HY_ORDER + 1) * HY_WIDTH
GQA_Q_COLS = GQA_HEADS * HEAD_DIM
DIFF_Q_COLS = DIFF_HEADS * 2 * DIFF_QK_DIM
GQA_KV_COLS = GQA_KV_HEADS * HEAD_DIM
DIFF_V_COLS = DIFF_HEADS * DIFF_V_DIM
KV_START = HY_COLS + GQA_Q_COLS + DIFF_Q_COLS
IN_COLS = KV_START + 2 * GQA_KV_COLS + DIFF_Q_COLS + DIFF_V_COLS

N_EXPERTS = 32
TOP_K = 4
EXPERT_FF = D_MODEL
SWIGLU_LIMIT = 7.0
SWIGLU_ALPHA = 1.702

kernel_name = 'hybrid_hyena_gqa_diffattn_moe_dit'


def _rmsnorm(x, g):
    xf = x.astype(jnp.float32)
    y = xf * lax.rsqrt(jnp.mean(xf * xf, axis=-1, keepdims=True) + NORM_EPS)
    return (y * g.astype(jnp.float32)).astype(x.dtype)


def _rope_1d(x, pos):
    half = x.shape[-1] // 2
    inv_freq = ROPE_THETA ** (-jnp.arange(half, dtype=jnp.float32) / half)
    ang = pos.astype(jnp.float32)[:, None] * inv_freq[None, :]
    shape = (pos.shape[0],) + (1,) * (x.ndim - 3) + (half,)
    cos = jnp.cos(ang).reshape(shape).astype(x.dtype)
    sin = jnp.sin(ang).reshape(shape).astype(x.dtype)
    x1, x2 = x[..., :half], x[..., half:]
    return jnp.concatenate([x1 * cos - x2 * sin, x1 * sin + x2 * cos], axis=-1)


def _rope_2d(x, row, col):
    h = x.shape[-1] // 2
    return jnp.concatenate([_rope_1d(x[..., :h], row), _rope_1d(x[..., h:], col)], axis=-1)


def _short_conv(u, w, b):
    L = u.shape[1]
    up = jnp.pad(u, ((0, 0), (1, 1), (0, 0)))
    return up[:, :L] * w[0] + up[:, 1:L + 1] * w[1] + up[:, 2:] * w[2] + b


def _hyena_filters(L, lp):
    f32 = jnp.float32
    t = jnp.arange(L, dtype=f32)
    bands = jnp.linspace(1e-4, HY_BANDS - 1, HY_BANDS, dtype=f32)
    w = (2.0 * math.pi / L) * t
    feats = jnp.concatenate([(t / L)[:, None], jnp.cos(w[:, None] * bands), -jnp.sin(w[:, None] * bands)], axis=-1)
    freq = lp['hy_freq'].astype(f32)
    h = jnp.sin(freq * (feats @ lp['hy_w1'].astype(f32) + lp['hy_b1'].astype(f32)))
    h = jnp.sin(freq * (h @ lp['hy_w2'].astype(f32) + lp['hy_b2'].astype(f32)))
    k = h @ lp['hy_w3'].astype(f32)
    offset = jnp.abs(t - L // 2) * (2.0 / L)
    k = k * jnp.exp(-offset[:, None] * jnp.abs(lp['hy_decay'].astype(f32)))
    k = k / jnp.sum(jnp.abs(k), axis=0, keepdims=True)
    return k.reshape(L, HY_ORDER, HY_WIDTH)


def _long_conv(u, k, bias):
    L = u.shape[1]
    spec = jnp.fft.rfft(u, n=2 * L, axis=1) * jnp.fft.rfft(k, n=2 * L, axis=0)[None]
    y = jnp.fft.irfft(spec, n=2 * L, axis=1)[:, L // 2: L // 2 + L]
    return y + u * bias


def _hyena(u, lp):
    parts = jnp.split(_short_conv(u, lp['hy_conv_w'], lp['hy_conv_b']).astype(jnp.float32), HY_ORDER + 1, axis=-1)
    filt = _hyena_filters(u.shape[1], lp)
    bias = lp['hy_bias'].astype(jnp.float32)
    y = parts[0]
    for n in range(HY_ORDER):
        y = parts[n + 1] * _long_conv(y, filt[:, n], bias[n])
    return y.astype(u.dtype)


def _gqa_attend(q, k, v):
    s = jnp.einsum('bqhgd,bkhd->bhgqk', q, k).astype(jnp.float32) * (HEAD_DIM ** -0.5)
    p = jax.nn.softmax(s, axis=-1).astype(v.dtype)
    return jnp.einsum('bhgqk,bkhd->bqhgd', p, v)


def _diff_attend(q, k, v, lam):
    s = jnp.einsum('bqhcd,bkhcd->bchqk', q, k).astype(jnp.float32) * (DIFF_QK_DIM ** -0.5)
    a = jax.nn.softmax(s, axis=-1)
    w = (a[:, 0] - lam * a[:, 1]).astype(v.dtype)
    return jnp.einsum('bhqk,bkhe->bqhe', w, v)


def _sweep_blocks(fn, q, *kv):
    B, L = q.shape[:2]
    nb = L // Q_BLOCK
    qb = jnp.moveaxis(q.reshape((B, nb, Q_BLOCK) + q.shape[2:]), 1, 0)
    out = lax.map(lambda blk: fn(blk, *kv), qb)
    out = jnp.moveaxis(out, 0, 1)
    return out.reshape((B, L) + out.shape[3:])


def _keys_values(pkv, lp, pos):
    B, L, _ = pkv.shape
    o1 = GQA_KV_COLS
    o2 = 2 * GQA_KV_COLS
    o3 = o2 + DIFF_Q_COLS
    gk = _rmsnorm(pkv[..., :o1].reshape(B, L, GQA_KV_HEADS, HEAD_DIM), lp['k_norm'])
    gv = pkv[..., o1:o2].reshape(B, L, GQA_KV_HEADS, HEAD_DIM)
    dk = pkv[..., o2:o3].reshape(B, L, DIFF_HEADS, 2, DIFF_QK_DIM)
    dv = pkv[..., o3:].reshape(B, L, DIFF_HEADS, DIFF_V_DIM)
    if pos is not None:
        gk = _rope_2d(gk, *pos)
        dk = _rope_2d(dk, *pos)
    return gk, gv, dk, dv


def _mix_stream(pq, kv, lp, lam, lam_init, pos):
    B, L, _ = pq.shape
    o1 = HY_COLS
    o2 = o1 + GQA_Q_COLS
    hy = _hyena(pq[..., :o1], lp)
    gq = _rmsnorm(pq[..., o1:o2].reshape(B, L, GQA_KV_HEADS, GQA_GROUP, HEAD_DIM), lp['q_norm'])
    dq = pq[..., o2:].reshape(B, L, DIFF_HEADS, 2, DIFF_QK_DIM)
    gk, gv, dk, dv = kv
    diff_fn = functools.partial(_diff_attend, lam=lam)
    if pos is None:
        go = _gqa_attend(gq, gk, gv)
        do = diff_fn(dq, dk, dv)
    else:
        gq = _rope_2d(gq, *pos)
        dq = _rope_2d(dq, *pos)
        go = _sweep_blocks(_gqa_attend, gq, gk, gv)
        do = _sweep_blocks(diff_fn, dq, dk, dv)
    do = _rmsnorm(do, lp['subln']) * (1.0 - lam_init)
    y = jnp.concatenate([hy, go.reshape(B, L, GQA_Q_COLS), do.reshape(B, L, DIFF_V_COLS)], axis=-1)
    return y @ lp['w_out']


def _moe(t, lp):
    logits = (t @ lp['w_router'] + lp['b_router']).astype(jnp.float32)
    top_v, top_i = lax.top_k(logits, TOP_K)
    wts = jax.nn.softmax(top_v, axis=-1)
    combine = jnp.sum(jax.nn.one_hot(top_i, N_EXPERTS, dtype=jnp.float32) * wts[..., None], axis=1).astype(t.dtype)
    y = jnp.zeros_like(t)
    for e in range(N_EXPERTS):
        gu = t @ lp['w_gu'][e] + lp['b_gu'][e]
        gate = jnp.minimum(gu[:, :EXPERT_FF], SWIGLU_LIMIT)
        up = jnp.clip(gu[:, EXPERT_FF:], -SWIGLU_LIMIT, SWIGLU_LIMIT)
        act = (up + 1.0) * gate * jax.nn.sigmoid(SWIGLU_ALPHA * gate)
        y = y + combine[:, e:e + 1] * (act @ lp['w_down'][e] + lp['b_down'][e])
    return y


def _layer(x, xc, c, c_ctx, lp, layer_idx, pos, ctx_out):
    D = D_MODEL
    f32 = jnp.float32
    mod = jax.nn.silu(c) @ lp['w_ada'] + lp['b_ada']
    sh1, sc1, g1, sh2, sc2, g2 = jnp.split(mod[:, None, :], 6, axis=-1)
    n_c = 6 if ctx_out else 2
    mod_c = jax.nn.silu(c_ctx) @ lp['w_ada'][:, :n_c * D] + lp['b_ada'][:n_c * D]
    cm = jnp.split(mod_c, n_c)
    lam_init = 0.8 - 0.6 * math.exp(-0.3 * layer_idx)
    lam = (jnp.exp(jnp.sum(lp['lam_q1'].astype(f32) * lp['lam_k1'].astype(f32)))
           - jnp.exp(jnp.sum(lp['lam_q2'].astype(f32) * lp['lam_k2'].astype(f32))) + lam_init)

    h = _rmsnorm(x, lp['g_pre_mix']) * (1.0 + sc1) + sh1
    hc = _rmsnorm(xc, lp['g_pre_mix']) * (1.0 + cm[1]) + cm[0]
    p = h @ lp['w_in']
    kv_c = _keys_values(hc @ lp['w_in'][:, KV_START:], lp, None)
    kv_l = _keys_values(p[..., KV_START:], lp, pos)
    kv_all = tuple(jnp.concatenate([a, b], axis=1) for a, b in zip(kv_c, kv_l))
    y = _mix_stream(p[..., :KV_START], kv_all, lp, lam, lam_init, pos)
    x = x + g1 * _rmsnorm(y, lp['g_post_mix'])

    if not ctx_out:
        h2 = _rmsnorm(x, lp['g_pre_ffn']) * (1.0 + sc2) + sh2
        y2 = _moe(h2.reshape(-1, D), lp).reshape(x.shape)
        return x + g2 * _rmsnorm(y2, lp['g_post_ffn']), None

    yc = _mix_stream(hc @ lp['w_in'][:, :KV_START], kv_c, lp, lam, lam_init, None)
    xc = xc + cm[2] * _rmsnorm(yc, lp['g_post_mix'])

    h2c = _rmsnorm(xc, lp['g_pre_ffn']) * (1.0 + cm[4]) + cm[3]
    h2 = _rmsnorm(x, lp['g_pre_ffn']) * (1.0 + sc2) + sh2
    tok = jnp.concatenate([h2c, h2], axis=1)
    y2 = _moe(tok.reshape(-1, D), lp).reshape(tok.shape)
    C = xc.shape[1]
    xc = xc + cm[5] * _rmsnorm(y2[:, :C], lp['g_post_ffn'])
    x = x + g2 * _rmsnorm(y2[:, C:], lp['g_post_ffn'])
    return x, xc


def setup_inputs(seed: int = 0) -> dict:
    key = jax.random.key(seed)
    ks = iter(jax.random.split(key, 40))
    f32 = jnp.float32
    L, D = DEPTH, D_MODEL

    def nrm(shape, scale=1.0):
        return scale * jax.random.normal(next(ks), shape, f32)

    def gain(shape):
        return 1.0 + nrm(shape, 0.05)

    decay0 = jnp.linspace(math.log(HY_TARGET) / HY_SLOW_DECAY, math.log(HY_TARGET) / HY_FAST_DECAY,
                          HY_ORDER * HY_WIDTH, dtype=f32)
    return {
        'x': nrm((BATCH, SEQ, D)),
        'c': nrm((BATCH, D)),
        'ctx': nrm((BATCH, CTX_LEN, D)),
        'c_ctx': nrm((D,)),
        'w_ada': nrm((L, D, 6 * D), 0.5 * D ** -0.5),
        'b_ada': nrm((L, 6 * D), 0.02),
        'g_pre_mix': gain((L, D)),
        'g_post_mix': gain((L, D)),
        'g_pre_ffn': gain((L, D)),
        'g_post_ffn': gain((L, D)),
        'w_in': nrm((L, D, IN_COLS), D ** -0.5),
        'w_out': nrm((L, MIX_WIDTH, D), MIX_WIDTH ** -0.5),
        'hy_conv_w': nrm((L, 3, HY_COLS), 3 ** -0.5),
        'hy_conv_b': nrm((L, HY_COLS), 0.02),
        'hy_w1': nrm((L, HY_EMB, HY_HIDDEN), HY_EMB ** -0.5),
        'hy_b1': nrm((L, HY_HIDDEN), 0.02),
        'hy_w2': nrm((L, HY_HIDDEN, HY_HIDDEN), HY_HIDDEN ** -0.5),
        'hy_b2': nrm((L, HY_HIDDEN), 0.02),
        'hy_w3': nrm((L, HY_HIDDEN, HY_ORDER * HY_WIDTH), HY_HIDDEN ** -0.5),
        'hy_freq': 1.0 + nrm((L, HY_HIDDEN), 0.01),
        'hy_decay': decay0 + nrm((L, HY_ORDER * HY_WIDTH), 0.01),
        'hy_bias': nrm((L, HY_ORDER, HY_WIDTH), 1.0),
        'q_norm': gain((L, HEAD_DIM)),
        'k_norm': gain((L, HEAD_DIM)),
        'lam_q1': nrm((L, DIFF_QK_DIM), 0.1),
        'lam_k1': nrm((L, DIFF_QK_DIM), 0.1),
        'lam_q2': nrm((L, DIFF_QK_DIM), 0.1),
        'lam_k2': nrm((L, DIFF_QK_DIM), 0.1),
        'subln': gain((L, DIFF_V_DIM)),
        'w_router': nrm((L, D, N_EXPERTS), D ** -0.5),
        'b_router': nrm((L, N_EXPERTS), 0.01),
        'w_gu': nrm((L, N_EXPERTS, D, 2 * EXPERT_FF), D ** -0.5),
        'b_gu': nrm((L, N_EXPERTS, 2 * EXPERT_FF), 0.01),
        'w_down': nrm((L, N_EXPERTS, EXPERT_FF, D), EXPERT_FF ** -0.5),
        'b_down': nrm((L, N_EXPERTS, D), 0.01),
    }


def reference(x, c, ctx, c_ctx, w_ada, b_ada, g_pre_mix, g_post_mix, g_pre_ffn, g_post_ffn,
              w_in, w_out, hy_conv_w, hy_conv_b, hy_w1, hy_b1, hy_w2, hy_b2, hy_w3, hy_freq,
              hy_decay, hy_bias, q_norm, k_norm, lam_q1, lam_k1, lam_q2, lam_k2, subln,
              w_router, b_router, w_gu, b_gu, w_down, b_down):
    n_tok = x.shape[1]
    rows = n_tok // GRID_W
    row = jnp.repeat(jnp.arange(rows, dtype=jnp.int32), GRID_W)
    col = jnp.tile(jnp.arange(GRID_W, dtype=jnp.int32), rows)
    xc = ctx
    for i in range(DEPTH):
        lp = {
            'w_ada': w_ada[i], 'b_ada': b_ada[i],
            'g_pre_mix': g_pre_mix[i], 'g_post_mix': g_post_mix[i],
            'g_pre_ffn': g_pre_ffn[i], 'g_post_ffn': g_post_ffn[i],
            'w_in': w_in[i], 'w_out': w_out[i],
            'hy_conv_w': hy_conv_w[i], 'hy_conv_b': hy_conv_b[i],
            'hy_w1': hy_w1[i], 'hy_b1': hy_b1[i], 'hy_w2': hy_w2[i], 'hy_b2': hy_b2[i],
            'hy_w3': hy_w3[i], 'hy_freq': hy_freq[i], 'hy_decay': hy_decay[i], 'hy_bias': hy_bias[i],
            'q_norm': q_norm[i], 'k_norm': k_norm[i],
            'lam_q1': lam_q1[i], 'lam_k1': lam_k1[i], 'lam_q2': lam_q2[i], 'lam_k2': lam_k2[i],
            'subln': subln[i],
            'w_router': w_router[i], 'b_router': b_router[i],
            'w_gu': w_gu[i], 'b_gu': b_gu[i], 'w_down': w_down[i], 'b_down': b_down[i],
        }
        x, xc = _layer(x, xc, c, c_ctx, lp, i, (row, col), i < DEPTH - 1)
    return x
```

```python
import functools
import math

import jax
import jax.numpy as jnp
from jax import lax
from jax.experimental import pallas as pl
from jax.experimental.pallas import tpu as pltpu

F32 = jnp.float32
BF16 = jnp.bfloat16
HIGHEST = lax.Precision.HIGHEST

HEAD_DIM = 64
ROPE_THETA = 10000.0
NORM_EPS = 1e-6
GRID_W = 64
HY_WIDTH = 256
HY_BANDS = 16
GQA_HEADS = 8
GQA_KV_HEADS = 2
GQA_GROUP = GQA_HEADS // GQA_KV_HEADS
DIFF_HEADS = 4
DIFF_QK_DIM = 32
DIFF_V_DIM = 64
TOP_K = 4
SWIGLU_LIMIT = 7.0
SWIGLU_ALPHA = 1.702

HY_COLS = 3 * HY_WIDTH
GQA_Q_COLS = GQA_HEADS * HEAD_DIM
DIFF_Q_COLS = DIFF_HEADS * 2 * DIFF_QK_DIM
GQA_KV_COLS = GQA_KV_HEADS * HEAD_DIM
DIFF_V_COLS = DIFF_HEADS * DIFF_V_DIM
Q_COLS = GQA_Q_COLS + DIFF_Q_COLS
KV_COLS = GQA_KV_COLS + DIFF_Q_COLS
Q_START = HY_COLS
K_START = Q_START + Q_COLS
IN_COLS = K_START + 2 * KV_COLS

LANES = 128
SUBLANES = 8
VMEM_LIMIT_BYTES = 56 * 1024 * 1024
NEG_BIG = -1e30

_GQA_HEAD_ORDER = tuple(g + GQA_GROUP * j for g in range(GQA_GROUP) for j in range(GQA_KV_HEADS))


def _cparams(*sem):
    return pltpu.CompilerParams(dimension_semantics=sem, vmem_limit_bytes=VMEM_LIMIT_BYTES)


def _rms(x):
    return x * lax.rsqrt(jnp.mean(x * x, axis=-1, keepdims=True) + NORM_EPS)


def _ada_body(c_ref, w_ref, b_ref, o_ref):
    c = c_ref[...]
    s = c / (1.0 + jnp.exp(-c))
    o_ref[0] = jnp.dot(s.astype(BF16), w_ref[0].astype(BF16), preferred_element_type=F32) + b_ref[0]


def _ada(cc, w_ada, b_ada):
    n_layers, d, n = w_ada.shape
    rows = cc.shape[0]
    tn = 512
    return pl.pallas_call(
        _ada_body,
        grid=(n_layers, n // tn),
        in_specs=[pl.BlockSpec((rows, d), lambda l, j: (0, 0)),
                  pl.BlockSpec((1, d, tn), lambda l, j: (l, 0, j)),
                  pl.BlockSpec((1, 1, tn), lambda l, j: (l, 0, j))],
        out_specs=pl.BlockSpec((1, rows, tn), lambda l, j: (l, 0, j)),
        out_shape=jax.ShapeDtypeStruct((n_layers, rows, n), F32),
        compiler_params=_cparams("parallel", "parallel"),
    )(cc, w_ada, b_ada.reshape(n_layers, 1, n))


def _rope(xc, cos, sin, half, lane):
    fwd = pltpu.roll(xc, LANES - half, 1)
    bwd = pltpu.roll(xc, half, 1)
    sw = jnp.where((lane % (2 * half)) < half, fwd, bwd)
    return xc * cos + sw * sin


def _inproj_body(x_ref, sc_ref, sh_ref, g_ref, w_ref, qg_ref, kg_ref, bd_ref,
                 cg_ref, sg_ref, cd_ref, sd_ref, hy_ref, q_ref, k_ref, v_ref):
    tm = x_ref.shape[1]
    h = _rms(x_ref[0]) * g_ref[...]
    h = h * (1.0 + sc_ref[0]) + sh_ref[0]
    p = jnp.dot(h.astype(BF16), w_ref[...], preferred_element_type=F32)
    hy_ref[0] = p[:, :HY_COLS]

    lane = lax.broadcasted_iota(jnp.int32, (tm, LANES), 1)
    cg, sg, cd, sd = cg_ref[...], sg_ref[...], cd_ref[...], sd_ref[...]
    bd = bd_ref[...]

    def head_norm(xc, gain):
        ms = jnp.dot(xc * xc, bd, precision=HIGHEST, preferred_element_type=F32) * (1.0 / HEAD_DIM)
        return xc * lax.rsqrt(ms + NORM_EPS) * gain

    for c in range(GQA_Q_COLS // LANES):
        xc = p[:, Q_START + c * LANES: Q_START + (c + 1) * LANES]
        xc = head_norm(xc, qg_ref[...])
        q_ref[0, :, c * LANES:(c + 1) * LANES] = _rope(xc, cg, sg, HEAD_DIM // 4, lane).astype(BF16)
    dscale = DIFF_QK_DIM ** -0.5
    for c in range(DIFF_Q_COLS // LANES):
        o = Q_START + GQA_Q_COLS + c * LANES
        xc = p[:, o:o + LANES] * dscale
        q_ref[0, :, GQA_Q_COLS + c * LANES: GQA_Q_COLS + (c + 1) * LANES] = (
            _rope(xc, cd, sd, DIFF_QK_DIM // 4, lane).astype(BF16))

    gk = head_norm(p[:, K_START:K_START + GQA_KV_COLS], kg_ref[...])
    k_ref[0, :, :GQA_KV_COLS] = _rope(gk, cg, sg, HEAD_DIM // 4, lane).astype(BF16)
    v_ref[0, :, :GQA_KV_COLS] = p[:, K_START + GQA_KV_COLS:K_START + 2 * GQA_KV_COLS].astype(BF16)
    dk0 = K_START + 2 * GQA_KV_COLS
    for c in range(DIFF_Q_COLS // LANES):
        xc = p[:, dk0 + c * LANES: dk0 + (c + 1) * LANES]
        k_ref[0, :, GQA_KV_COLS + c * LANES: GQA_KV_COLS + (c + 1) * LANES] = (
            _rope(xc, cd, sd, DIFF_QK_DIM // 4, lane).astype(BF16))
    v_ref[0, :, GQA_KV_COLS:] = p[:, dk0 + DIFF_Q_COLS:].astype(BF16)


def _inproj(x, scale, shift, g, w, qg, kg, bd, tables, tm):
    b, t, d = x.shape
    row = lambda bi, i: (bi, i, 0)
    per_b = lambda bi, i: (bi, 0, 0)
    const = lambda bi, i: (0, 0)
    tab = pl.BlockSpec((tm, LANES), lambda bi, i: (i, 0))
    return pl.pallas_call(
        _inproj_body,
        grid=(b, t // tm),
        in_specs=[pl.BlockSpec((1, tm, d), row),
                  pl.BlockSpec((1, 1, d), per_b), pl.BlockSpec((1, 1, d), per_b),
                  pl.BlockSpec((1, d), const),
                  pl.BlockSpec((d, IN_COLS), const),
                  pl.BlockSpec((1, LANES), const), pl.BlockSpec((1, LANES), const),
                  pl.BlockSpec((LANES, LANES), const),
                  tab, tab, tab, tab],
        out_specs=[pl.BlockSpec((1, tm, HY_COLS), row), pl.BlockSpec((1, tm, Q_COLS), row),
                   pl.BlockSpec((1, tm, KV_COLS), row), pl.BlockSpec((1, tm, KV_COLS), row)],
        out_shape=[jax.ShapeDtypeStruct((b, t, HY_COLS), F32), jax.ShapeDtypeStruct((b, t, Q_COLS), BF16),
                   jax.ShapeDtypeStruct((b, t, KV_COLS), BF16), jax.ShapeDtypeStruct((b, t, KV_COLS), BF16)],
        compiler_params=_cparams("parallel", "parallel"),
    )(x, scale, shift, g, w, qg, kg, bd, *tables)


def _filter_body(f_ref, w1_ref, b1_ref, w2_ref, b2_ref, w3_ref, fr_ref, dc_ref, o_ref):
    n = f_ref.shape[0]
    fr = fr_ref[...]
    h = jnp.sin(fr * (jnp.dot(f_ref[...], w1_ref[...], precision=HIGHEST, preferred_element_type=F32) + b1_ref[...]))
    h = jnp.sin(fr * (jnp.dot(h, w2_ref[...], precision=HIGHEST, preferred_element_type=F32) + b2_ref[...]))
    k = jnp.dot(h, w3_ref[...], precision=HIGHEST, preferred_element_type=F32)
    t = lax.broadcasted_iota(jnp.int32, k.shape, 0)
    off = jnp.abs(t - n // 2).astype(F32) * (2.0 / n)
    k = k * jnp.exp(-off * jnp.abs(dc_ref[...]))
    k = k / jnp.sum(jnp.abs(k), axis=0, keepdims=True)
    o_ref[...] = k.astype(BF16)


def _hyena_filters(feats, w1p, b1, w2, b2, w3, freq, decay):
    n = feats.shape[0]
    hid = w2.shape[0]
    cols = w3.shape[1]
    const = lambda j: (0, 0)
    return pl.pallas_call(
        _filter_body,
        grid=(cols // LANES,),
        in_specs=[pl.BlockSpec((n, LANES), const), pl.BlockSpec((LANES, hid), const),
                  pl.BlockSpec((1, hid), const), pl.BlockSpec((hid, hid), const), pl.BlockSpec((1, hid), const),
                  pl.BlockSpec((hid, LANES), lambda j: (0, j)), pl.BlockSpec((1, hid), const),
                  pl.BlockSpec((1, LANES), lambda j: (0, j))],
        out_specs=pl.BlockSpec((n, LANES), lambda j: (0, j)),
        out_shape=jax.ShapeDtypeStruct((n, cols), BF16),
        compiler_params=_cparams("parallel"),
    )(feats, w1p, b1, w2, b2, w3, freq, decay)


def _hyprep_body(p_ref, prev_ref, next_ref, w_ref, b_ref, v_ref, vb_ref, x1_ref, x2_ref, buf):
    i = pl.program_id(1)
    n = pl.num_programs(1)
    tm = p_ref.shape[1]
    buf[SUBLANES:tm + SUBLANES, :] = p_ref[0]
    buf[SUBLANES - 1:SUBLANES, :] = jnp.where(i > 0, prev_ref[0][SUBLANES - 1:SUBLANES, :], 0.0)
    buf[tm + SUBLANES:tm + SUBLANES + 1, :] = jnp.where(i < n - 1, next_ref[0][0:1, :], 0.0)
    w = w_ref[...]
    y = (buf[SUBLANES - 1:tm + SUBLANES - 1, :] * w[0:1] + buf[SUBLANES:tm + SUBLANES, :] * w[1:2]
         + buf[SUBLANES + 1:tm + SUBLANES + 1, :] * w[2:3] + b_ref[...])
    v_ref[...] = y[:, :HY_WIDTH]
    vb_ref[...] = y[:, :HY_WIDTH].astype(BF16)
    x1_ref[...] = y[:, HY_WIDTH:2 * HY_WIDTH]
    x2_ref[...] = y[:, 2 * HY_WIDTH:]


def _hyprep(p_hy, conv_w, conv_b, tm):
    b, t, _ = p_hy.shape
    nb8 = t // SUBLANES
    r8 = tm // SUBLANES
    tmaj = pl.BlockSpec((tm, HY_WIDTH), lambda bi, i: (i, bi))
    shp = lambda dt: jax.ShapeDtypeStruct((t, b * HY_WIDTH), dt)
    return pl.pallas_call(
        _hyprep_body,
        grid=(b, t // tm),
        in_specs=[pl.BlockSpec((1, tm, HY_COLS), lambda bi, i: (bi, i, 0)),
                  pl.BlockSpec((1, SUBLANES, HY_COLS), lambda bi, i: (bi, jnp.maximum(i * r8 - 1, 0), 0)),
                  pl.BlockSpec((1, SUBLANES, HY_COLS), lambda bi, i: (bi, jnp.minimum((i + 1) * r8, nb8 - 1), 0)),
                  pl.BlockSpec((3, HY_COLS), lambda bi, i: (0, 0)),
                  pl.BlockSpec((1, HY_COLS), lambda bi, i: (0, 0))],
        out_specs=[tmaj, tmaj, tmaj, tmaj],
        out_shape=[shp(F32), shp(BF16), shp(F32), shp(F32)],
        scratch_shapes=[pltpu.VMEM((tm + 2 * SUBLANES, HY_COLS), F32)],
        compiler_params=_cparams("parallel", "parallel"),
    )(p_hy, p_hy, p_hy, conv_w, conv_b)


def _dft_mats(n_t):
    n_fft = 3 * n_t // 2
    half = n_fft // 2
    t = jnp.arange(n_t, dtype=jnp.int32)
    f = jnp.arange(half, dtype=jnp.int32)
    step = 2.0 * math.pi / n_fft
    ang = ((f[:, None] * t[None, :]) % n_fft).astype(F32) * step
    alt_t = (1 - 2 * (t % 2)).astype(F32)
    f_re = jnp.cos(ang)
    f_im = jnp.where(f[:, None] == 0, alt_t[None, :], -jnp.sin(ang))
    fwd = jnp.concatenate([f_re, f_im], axis=0).astype(BF16)
    n_out = t + n_t // 2
    ang2 = ((n_out[:, None] * f[None, :]) % n_fft).astype(F32) * step
    alt_n = (1 - 2 * (n_out % 2)).astype(F32)
    g_re = jnp.where(f[None, :] == 0, 1.0, 2.0 * jnp.cos(ang2)).astype(BF16)
    g_im = jnp.where(f[None, :] == 0, alt_n[:, None], -2.0 * jnp.sin(ang2)).astype(BF16)
    return fwd, g_re, g_im


def _dft_fwd_body(fre_ref, fim_ref, u_ref, zre_ref, zim_ref):
    u = u_ref[...]
    zre_ref[...] = jnp.dot(fre_ref[...], u, preferred_element_type=F32)
    zim_ref[...] = jnp.dot(fim_ref[...], u, preferred_element_type=F32)


def _dft_fwd_mul_body(fre_ref, fim_ref, u_ref, kre_ref, kim_ref, zre_ref, zim_ref):
    u = u_ref[...]
    xr = jnp.dot(fre_ref[...], u, preferred_element_type=F32)
    xi = jnp.dot(fim_ref[...], u, preferred_element_type=F32)
    kre, kim = kre_ref[...], kim_ref[...]
    row0 = (lax.broadcasted_iota(jnp.int32, xr.shape, 0) == 0) & (pl.program_id(0) == 0)
    a = xr * kre
    bb = xi * kim
    zre_ref[...] = (a - jnp.where(row0, 0.0, bb)).astype(zre_ref.dtype)
    zim_ref[...] = jnp.where(row0, bb, xr * kim + xi * kre).astype(zim_ref.dtype)


def _dft_fwd(fwd, u, tm, kf=None, order=0):
    n_fft, n_t = fwd.shape
    half = n_fft // 2
    cols = u.shape[1]
    nblk = half // tm
    tn = HY_WIDTH
    in_specs = [pl.BlockSpec((tm, n_t), lambda i, j: (i, 0)),
                pl.BlockSpec((tm, n_t), lambda i, j: (i + nblk, 0)),
                pl.BlockSpec((n_t, tn), lambda i, j: (0, j))]
    args = [fwd, fwd, u]
    if kf is None:
        body, odt = _dft_fwd_body, F32
    else:
        body, odt = _dft_fwd_mul_body, BF16
        in_specs += [pl.BlockSpec((tm, tn), lambda i, j: (i, order)), pl.BlockSpec((tm, tn), lambda i, j: (i, order))]
        args += [kf[0], kf[1]]
    out = pl.BlockSpec((tm, tn), lambda i, j: (i, j))
    return pl.pallas_call(
        body,
        grid=(nblk, cols // tn),
        in_specs=in_specs,
        out_specs=[out, out],
        out_shape=[jax.ShapeDtypeStruct((half, cols), odt)] * 2,
        compiler_params=_cparams("parallel", "parallel"),
    )(*args)


def _dft_inv_body(gre_ref, gim_ref, zre_ref, zim_ref, u_ref, x_ref, bias_ref, *out_refs, inv_n):
    y = (jnp.dot(gre_ref[...], zre_ref[...], preferred_element_type=F32)
         + jnp.dot(gim_ref[...], zim_ref[...], preferred_element_type=F32)) * inv_n
    o = x_ref[...] * (y + u_ref[...] * bias_ref[...])
    if len(out_refs) == 2:
        out_refs[0][...] = o
        out_refs[1][...] = o.astype(BF16)
    else:
        out_refs[0][0] = o.astype(BF16)


def _dft_inv(g_re, g_im, zre, zim, u, xg, bias, tm, batch, last):
    n_t, half = g_re.shape
    cols = u.shape[1]
    tn = HY_WIDTH
    gsp = pl.BlockSpec((tm, half), lambda i, j: (i, 0))
    zsp = pl.BlockSpec((half, tn), lambda i, j: (0, j))
    tsp = pl.BlockSpec((tm, tn), lambda i, j: (i, j))
    if last:
        out_specs = [pl.BlockSpec((1, tm, tn), lambda i, j: (j, i, 0))]
        out_shape = [jax.ShapeDtypeStruct((batch, n_t, tn), BF16)]
    else:
        out_specs = [tsp, tsp]
        out_shape = [jax.ShapeDtypeStruct((n_t, cols), F32), jax.ShapeDtypeStruct((n_t, cols), BF16)]
    return pl.pallas_call(
        functools.partial(_dft_inv_body, inv_n=1.0 / (2 * half)),
        grid=(n_t // tm, cols // tn),
        in_specs=[gsp, gsp, zsp, zsp, tsp, tsp, pl.BlockSpec((1, tn), lambda i, j: (0, 0))],
        out_specs=out_specs,
        out_shape=out_shape,
        compiler_params=_cparams("parallel", "parallel"),
    )(g_re, g_im, zre, zim, u, xg, bias)


def _hyena(p_hy, conv_w, conv_b, hy_bias, kf, mats, tm_prep, tm_f, tm_t):
    b = p_hy.shape[0]
    fwd, g_re, g_im = mats
    v, vb, x1, x2 = _hyprep(p_hy, conv_w, conv_b, tm_prep)
    zre, zim = _dft_fwd(fwd, vb, tm_f, kf, 0)
    y1, y1b = _dft_inv(g_re, g_im, zre, zim, v, x1, hy_bias[0:1], tm_t, b, False)
    zre, zim = _dft_fwd(fwd, y1b, tm_f, kf, 1)
    (hy,) = _dft_inv(g_re, g_im, zre, zim, y1, x2, hy_bias[1:2], tm_t, b, True)
    return hy


def _attn_body(q_ref, k_ref, v_ref, lam_ref, sub_ref, go_ref, do_ref, *, kb, lam_init):
    tq = q_ref.shape[1]
    nkb = k_ref.shape[1] // kb
    lp = lam_ref[...]
    lam = (jnp.exp(jnp.sum(lp[0:1] * lp[1:2], axis=1, keepdims=True))
           - jnp.exp(jnp.sum(lp[2:3] * lp[3:4], axis=1, keepdims=True)) + lam_init)

    def attend(qm, col, width):
        def body(i, carry):
            m, l, acc = carry
            off = pl.multiple_of(i * kb, kb)
            kblk = k_ref[0, pl.ds(off, kb), col:col + width]
            vblk = v_ref[0, pl.ds(off, kb), col:col + width]
            s = lax.dot_general(qm, kblk, (((1,), (1,)), ((), ())), preferred_element_type=F32)
            m_new = jnp.maximum(m, jnp.max(s, axis=1, keepdims=True))
            alpha = jnp.exp(m - m_new)
            p = jnp.exp(s - m_new)
            l = alpha * l + jnp.sum(p, axis=1, keepdims=True)
            acc = alpha * acc + jnp.dot(p.astype(BF16), vblk, preferred_element_type=F32)
            return m_new, l, acc
        init = (jnp.full((tq, 1), NEG_BIG, F32), jnp.zeros((tq, 1), F32), jnp.zeros((tq, width), F32))
        _, l, acc = lax.fori_loop(0, nkb, body, init)
        return acc / l

    lane = lax.broadcasted_iota(jnp.int32, (tq, LANES), 1)
    for c in range(GQA_Q_COLS // LANES):
        qc = q_ref[0, :, c * LANES:(c + 1) * LANES]
        outs = []
        for j in range(GQA_KV_HEADS):
            qm = jnp.where((lane // HEAD_DIM) == j, qc, jnp.zeros_like(qc))
            outs.append(attend(qm, 0, GQA_KV_COLS))
        go_ref[0, :, c * LANES:(c + 1) * LANES] = jnp.where(lane < HEAD_DIM, outs[0], outs[1]).astype(BF16)

    lane_d = lax.broadcasted_iota(jnp.int32, (tq, DIFF_Q_COLS), 1)
    dq = q_ref[0, :, GQA_Q_COLS:]
    sub = sub_ref[...] * (1.0 - lam_init)
    out = jnp.zeros((tq, DIFF_V_COLS), F32)
    for h in range(DIFF_HEADS):
        maps = []
        for cmap in range(2):
            qm = jnp.where((lane_d // DIFF_QK_DIM) == 2 * h + cmap, dq, jnp.zeros_like(dq))
            maps.append(attend(qm, GQA_KV_COLS, DIFF_Q_COLS))
        d = maps[0] - lam * maps[1]
        mine = (lane_d // DIFF_V_DIM) == h
        ms = jnp.sum(jnp.where(mine, d * d, 0.0), axis=1, keepdims=True) * (1.0 / DIFF_V_DIM)
        out = jnp.where(mine, d * lax.rsqrt(ms + NORM_EPS) * sub, out)
    do_ref[0] = out.astype(BF16)


def _attention(q, k, v, lam_p, sub, lam_init, tq, kb):
    b, t, _ = q.shape
    lk = k.shape[1]
    row = lambda bi, i: (bi, i, 0)
    per_b = lambda bi, i: (bi, 0, 0)
    return pl.pallas_call(
        functools.partial(_attn_body, kb=kb, lam_init=lam_init),
        grid=(b, t // tq),
        in_specs=[pl.BlockSpec((1, tq, Q_COLS), row),
                  pl.BlockSpec((1, lk, KV_COLS), per_b), pl.BlockSpec((1, lk, KV_COLS), per_b),
                  pl.BlockSpec(lam_p.shape, lambda bi, i: (0, 0)),
                  pl.BlockSpec((1, DIFF_V_COLS), lambda bi, i: (0, 0))],
        out_specs=[pl.BlockSpec((1, tq, GQA_Q_COLS), row), pl.BlockSpec((1, tq, DIFF_V_COLS), row)],
        out_shape=[jax.ShapeDtypeStruct((b, t, GQA_Q_COLS), BF16), jax.ShapeDtypeStruct((b, t, DIFF_V_COLS), BF16)],
        compiler_params=_cparams("parallel", "parallel"),
    )(q, k, v, lam_p, sub)


def _outproj_body(hy_ref, go_ref, do_ref, w_ref, x_ref, g1_ref, gpost_ref, gpre_ref, sc_ref, sh_ref,
                  wr_ref, br_ref, xm_ref, h2_ref, comb_ref):
    y = (jnp.dot(hy_ref[0], w_ref[:HY_WIDTH], preferred_element_type=F32)
         + jnp.dot(go_ref[0], w_ref[HY_WIDTH:HY_WIDTH + GQA_Q_COLS], preferred_element_type=F32)
         + jnp.dot(do_ref[0], w_ref[HY_WIDTH + GQA_Q_COLS:], preferred_element_type=F32))
    xm = x_ref[0] + g1_ref[0] * (_rms(y) * gpost_ref[...])
    xm_ref[0] = xm
    h2 = _rms(xm) * gpre_ref[...] * (1.0 + sc_ref[0]) + sh_ref[0]
    h2_ref[0] = h2.astype(BF16)

    logits = jnp.dot(h2, wr_ref[...], precision=HIGHEST, preferred_element_type=F32) + br_ref[...]
    lane = lax.broadcasted_iota(jnp.int32, logits.shape, 1)
    comb = jnp.zeros_like(logits)
    denom = jnp.zeros((logits.shape[0], 1), F32)
    top = None
    for _ in range(TOP_K):
        mx = jnp.max(logits, axis=1, keepdims=True)
        first = jnp.min(jnp.where(logits == mx, lane, LANES), axis=1, keepdims=True)
        sel = lane == first
        top = mx if top is None else top
        e = jnp.exp(mx - top)
        comb = jnp.where(sel, e, comb)
        denom = denom + e
        logits = jnp.where(sel, NEG_BIG, logits)
    comb_ref[0] = comb / denom


def _outproj(hy, go, do, w_out, x, g1, gpost, gpre, sc2, sh2, wr, br, tm):
    b, t, d = x.shape
    row = lambda bi, i: (bi, i, 0)
    per_b = lambda bi, i: (bi, 0, 0)
    const = lambda bi, i: (0, 0)
    vec = pl.BlockSpec((1, 1, d), per_b)
    return pl.pallas_call(
        _outproj_body,
        grid=(b, t // tm),
        in_specs=[pl.BlockSpec((1, tm, HY_WIDTH), row), pl.BlockSpec((1, tm, GQA_Q_COLS), row),
                  pl.BlockSpec((1, tm, DIFF_V_COLS), row), pl.BlockSpec(w_out.shape, const),
                  pl.BlockSpec((1, tm, d), row), vec, pl.BlockSpec((1, d), const), pl.BlockSpec((1, d), const),
                  vec, vec, pl.BlockSpec((d, LANES), const), pl.BlockSpec((1, LANES), const)],
        out_specs=[pl.BlockSpec((1, tm, d), row), pl.BlockSpec((1, tm, d), row), pl.BlockSpec((1, tm, LANES), row)],
        out_shape=[jax.ShapeDtypeStruct((b, t, d), F32), jax.ShapeDtypeStruct((b, t, d), BF16),
                   jax.ShapeDtypeStruct((b, t, LANES), F32)],
        compiler_params=_cparams("parallel", "parallel"),
    )(hy, go, do, w_out, x, g1, gpost, gpre, sc2, sh2, wr, br)


def _moe_body(h_ref, comb_ref, wgu_ref, bgu_ref, wd_ref, bd_ref, x_ref, g2_ref, gpost_ref, o_ref, acc_ref):
    e = pl.program_id(1)
    ff = wd_ref.shape[1]

    @pl.when(e == 0)
    def _():
        acc_ref[...] = jnp.zeros_like(acc_ref)

    gu = jnp.dot(h_ref[...], wgu_ref[0], preferred_element_type=F32) + bgu_ref[0]
    gate = jnp.minimum(gu[:, :ff], SWIGLU_LIMIT)
    up = jnp.clip(gu[:, ff:], -SWIGLU_LIMIT, SWIGLU_LIMIT)
    act = (up + 1.0) * gate / (1.0 + jnp.exp(-SWIGLU_ALPHA * gate))
    y = jnp.dot(act.astype(BF16), wd_ref[0], preferred_element_type=F32) + bd_ref[0]
    comb = comb_ref[...]
    lane = lax.broadcasted_iota(jnp.int32, comb.shape, 1)
    cw = jnp.sum(jnp.where(lane == e, comb, 0.0), axis=1, keepdims=True)
    acc_ref[...] += cw * y

    @pl.when(e == pl.num_programs(1) - 1)
    def _():
        o_ref[...] = x_ref[...] + g2_ref[0] * (_rms(acc_ref[...]) * gpost_ref[...])


def _moe(h2, comb, wgu, bgu, wd, bd, xm, g2, gpost, tm):
    b, t, d = xm.shape
    n_exp, _, ff2 = wgu.shape
    ff = ff2 // 2
    n = b * t
    per_tile = t // tm
    tok = lambda i, e: (i, 0)
    out = pl.pallas_call(
        _moe_body,
        grid=(n // tm, n_exp),
        in_specs=[pl.BlockSpec((tm, d), tok), pl.BlockSpec((tm, LANES), tok),
                  pl.BlockSpec((1, d, ff2), lambda i, e: (e, 0, 0)), pl.BlockSpec((1, 1, ff2), lambda i, e: (e, 0, 0)),
                  pl.BlockSpec((1, ff, d), lambda i, e: (e, 0, 0)), pl.BlockSpec((1, 1, d), lambda i, e: (e, 0, 0)),
                  pl.BlockSpec((tm, d), tok), pl.BlockSpec((1, 1, d), lambda i, e: (i // per_tile, 0, 0)),
                  pl.BlockSpec((1, d), lambda i, e: (0, 0))],
        out_specs=pl.BlockSpec((tm, d), tok),
        out_shape=jax.ShapeDtypeStruct((n, d), F32),
        scratch_shapes=[pltpu.VMEM((tm, d), F32)],
        compiler_params=_cparams("parallel", "arbitrary"),
    )(h2.reshape(n, d), comb.reshape(n, LANES), wgu, bgu.reshape(n_exp, 1, ff2), wd, bd.reshape(n_exp, 1, d),
      xm.reshape(n, d), g2, gpost)
    return out.reshape(b, t, d)


def _rope_tables(n_tok, dim, use_pos):
    if not use_pos:
        return jnp.ones((n_tok, LANES), F32), jnp.zeros((n_tok, LANES), F32)
    half = dim // 4
    t = jnp.arange(n_tok, dtype=jnp.int32)
    row = (t // GRID_W).astype(F32)
    col = (t % GRID_W).astype(F32)
    inv_freq = ROPE_THETA ** (-jnp.arange(half, dtype=F32) / half)
    a_row = row[:, None] * inv_freq[None, :]
    a_col = col[:, None] * inv_freq[None, :]
    cos = jnp.concatenate([jnp.cos(a_row)] * 2 + [jnp.cos(a_col)] * 2, axis=1)
    sin = jnp.concatenate([-jnp.sin(a_row), jnp.sin(a_row), -jnp.sin(a_col), jnp.sin(a_col)], axis=1)
    reps = LANES // dim
    return jnp.tile(cos, (1, reps)), jnp.tile(sin, (1, reps))


def _hyena_feats(n_t):
    t = jnp.arange(n_t, dtype=F32)
    bands = jnp.linspace(1e-4, HY_BANDS - 1, HY_BANDS, dtype=F32)
    w = (2.0 * math.pi / n_t) * t
    feats = jnp.concatenate([(t / n_t)[:, None], jnp.cos(w[:, None] * bands), -jnp.sin(w[:, None] * bands)], axis=-1)
    return jnp.pad(feats, ((0, 0), (0, LANES - feats.shape[1])))


def _tile_rows(t, cap):
    return cap if t % cap == 0 else t


def kernel(x, c, ctx, c_ctx, w_ada, b_ada, g_pre_mix, g_post_mix, g_pre_ffn, g_post_ffn, w_in, w_out, hy_conv_w,
           hy_conv_b, hy_w1, hy_b1, hy_w2, hy_b2, hy_w3, hy_freq, hy_decay, hy_bias, q_norm, k_norm, lam_q1, lam_k1,
           lam_q2, lam_k2, subln, w_router, b_router, w_gu, b_gu, w_down, b_down):
    b, s, d = x.shape
    n_ctx = ctx.shape[1]
    depth = w_ada.shape[0]
    n_exp = w_router.shape[-1]

    rows = -(-(b + 1) // SUBLANES) * SUBLANES
    cc = jnp.zeros((rows, d), F32).at[:b].set(c).at[b].set(c_ctx)
    mod = _ada(cc, w_ada, b_ada)

    head_cols = jnp.concatenate([jnp.arange(HEAD_DIM) + HEAD_DIM * h for h in _GQA_HEAD_ORDER])
    in_perm = jnp.concatenate([jnp.arange(Q_START), Q_START + head_cols, jnp.arange(Q_START + GQA_Q_COLS, IN_COLS)])
    out_perm = jnp.concatenate([jnp.arange(HY_WIDTH), HY_WIDTH + head_cols, jnp.arange(HY_WIDTH + GQA_Q_COLS, d)])
    w_in_b = w_in[:, :, in_perm].astype(BF16)
    w_out_b = w_out[:, out_perm, :].astype(BF16)
    w_gu_b = w_gu.astype(BF16)
    w_down_b = w_down.astype(BF16)
    wr_p = jnp.pad(w_router, ((0, 0), (0, 0), (0, LANES - n_exp)))
    br_p = jnp.pad(b_router, ((0, 0), (0, LANES - n_exp)), constant_values=NEG_BIG)[:, None, :]

    lane = jnp.arange(LANES)
    block_diag = (lane[:, None] // HEAD_DIM == lane[None, :] // HEAD_DIM).astype(F32)
    reps = LANES // HEAD_DIM

    tabs_lat = _rope_tables(s, HEAD_DIM, True) + _rope_tables(s, DIFF_QK_DIM, True)
    tabs_ctx = _rope_tables(n_ctx, HEAD_DIM, False) + _rope_tables(n_ctx, DIFF_QK_DIM, False)
    mats_lat = _dft_mats(s)
    feats_lat = _hyena_feats(s)
    if depth > 1:
        mats_ctx = _dft_mats(n_ctx)
        feats_ctx = _hyena_feats(n_ctx)

    tm_lat = _tile_rows(s, 512)
    tm_ctx = _tile_rows(n_ctx, 256)
    lk = n_ctx + s
    kb = 256 if (lk % 256 == 0 and n_ctx % 256 == 0) else 128

    def filters(i, feats, mats):
        w1p = jnp.pad(hy_w1[i], ((0, LANES - hy_w1.shape[1]), (0, 0)))
        k = _hyena_filters(feats, w1p, hy_b1[i][None], hy_w2[i], hy_b2[i][None], hy_w3[i], hy_freq[i][None],
                           hy_decay[i][None])
        half = mats[0].shape[0] // 2
        return _dft_fwd(mats[0], k, _tile_rows(half, 512))

    xc = ctx
    for i in range(depth):
        ctx_out = i < depth - 1
        lam_init = 0.8 - 0.6 * math.exp(-0.3 * i)
        m_lat = mod[i, :b].reshape(b, 1, 6 * d)
        sh1, sc1, g1, sh2, sc2, g2 = [m_lat[:, :, j * d:(j + 1) * d] for j in range(6)]
        m_ctx = jnp.broadcast_to(mod[i, b].reshape(1, 1, 6 * d), (b, 1, 6 * d))
        csh1, csc1, cg1, csh2, csc2, cg2 = [m_ctx[:, :, j * d:(j + 1) * d] for j in range(6)]

        qg = jnp.tile(q_norm[i], reps)[None] * (HEAD_DIM ** -0.5)
        kg = jnp.tile(k_norm[i], reps)[None]
        lam_p = jnp.stack([lam_q1[i], lam_k1[i], lam_q2[i], lam_k2[i]])
        sub = jnp.tile(subln[i], DIFF_HEADS)[None]
        proj = functools.partial(_inproj, g=g_pre_mix[i][None], w=w_in_b[i], qg=qg, kg=kg, bd=block_diag)

        hy_c, q_c, k_c, v_c = proj(xc, csc1, csh1, tables=tabs_ctx, tm=tm_ctx)
        hy_l, q_l, k_l, v_l = proj(x, sc1, sh1, tables=tabs_lat, tm=tm_lat)
        k_all = jnp.concatenate([k_c, k_l], axis=1)
        v_all = jnp.concatenate([v_c, v_l], axis=1)

        attn = functools.partial(_attention, lam_p=lam_p, sub=sub, lam_init=lam_init)
        go_l, do_l = attn(q_l, k_all, v_all, tq=_tile_rows(s, 256), kb=kb)
        kf_lat = filters(i, feats_lat, mats_lat)
        half_lat = mats_lat[0].shape[0] // 2
        hyo_l = _hyena(hy_l, hy_conv_w[i], hy_conv_b[i][None], hy_bias[i], kf_lat, mats_lat,
                       tm_lat, _tile_rows(half_lat, 512), tm_lat)

        post = functools.partial(_outproj, w_out=w_out_b[i], gpost=g_post_mix[i][None], gpre=g_pre_ffn[i][None],
                                 wr=wr_p[i], br=br_p[i])
        moe = functools.partial(_moe, wgu=w_gu_b[i], bgu=b_gu[i], wd=w_down_b[i], bd=b_down[i],
                                gpost=g_post_ffn[i][None])

        xm, h2, comb = post(hyo_l, go_l, do_l, x=x, g1=g1, sc2=sc2, sh2=sh2, tm=tm_lat)
        x = moe(h2, comb, xm=xm, g2=g2, tm=_tile_rows(s, 512))

        if ctx_out:
            go_c, do_c = attn(q_c, k_c, v_c, tq=tm_ctx, kb=128 if n_ctx % 256 else 256)
            kf_ctx = filters(i, feats_ctx, mats_ctx)
            half_ctx = mats_ctx[0].shape[0] // 2
            hyo_c = _hyena(hy_c, hy_conv_w[i], hy_conv_b[i][None], hy_bias[i], kf_ctx, mats_ctx,
                           tm_ctx, _tile_rows(half_ctx, 512), tm_ctx)
            xcm, h2c, combc = post(hyo_c, go_c, do_c, x=xc, g1=cg1, sc2=csc2, sh2=csh2, tm=tm_ctx)
            xc = moe(h2c, combc, xm=xcm, g2=cg2, tm=tm_ctx)
    return x
```

```python
import functools
import math

import jax
import jax.numpy as jnp
from jax import lax
from jax.experimental import pallas as pl
from jax.experimental.pallas import tpu as pltpu

F32 = jnp.float32
BF16 = jnp.bfloat16
HIGHEST = lax.Precision.HIGHEST

HEAD_DIM = 64
ROPE_THETA = 10000.0
NORM_EPS = 1e-6
GRID_W = 64
HY_WIDTH = 256
HY_BANDS = 16
GQA_HEADS = 8
GQA_KV_HEADS = 2
GQA_GROUP = GQA_HEADS // GQA_KV_HEADS
DIFF_HEADS = 4
DIFF_QK_DIM = 32
DIFF_V_DIM = 64
TOP_K = 4
SWIGLU_LIMIT = 7.0
SWIGLU_ALPHA = 1.702

HY_COLS = 3 * HY_WIDTH
GQA_Q_COLS = GQA_HEADS * HEAD_DIM
DIFF_Q_COLS = DIFF_HEADS * 2 * DIFF_QK_DIM
GQA_KV_COLS = GQA_KV_HEADS * HEAD_DIM
DIFF_V_COLS = DIFF_HEADS * DIFF_V_DIM
Q_COLS = GQA_Q_COLS + DIFF_Q_COLS
KV_COLS = GQA_KV_COLS + DIFF_Q_COLS
Q_START = HY_COLS
K_START = Q_START + Q_COLS
IN_COLS = K_START + 2 * KV_COLS
V_HEAD_ROWS = HEAD_DIM + 16
VT_ROWS = (KV_COLS // HEAD_DIM) * V_HEAD_ROWS

LANES = 128
SUBLANES = 8
VMEM_LIMIT_BYTES = 56 * 1024 * 1024
NEG_BIG = -1e30
LOG2E = math.log2(math.e)

_GQA_HEAD_ORDER = tuple(g + GQA_GROUP * j for g in range(GQA_GROUP) for j in range(GQA_KV_HEADS))


def _cparams(*sem):
    return pltpu.CompilerParams(dimension_semantics=sem, vmem_limit_bytes=VMEM_LIMIT_BYTES)


def _rms(x):
    return x * lax.rsqrt(jnp.mean(x * x, axis=-1, keepdims=True) + NORM_EPS)


def _ada_body(c_ref, w_ref, b_ref, o_ref):
    c = c_ref[...]
    s = c / (1.0 + jnp.exp(-c))
    o_ref[0] = jnp.dot(s.astype(BF16), w_ref[0].astype(BF16), preferred_element_type=F32) + b_ref[0]


def _ada(cc, w_ada, b_ada):
    n_layers, d, n = w_ada.shape
    rows = cc.shape[0]
    tn = 512
    return pl.pallas_call(
        _ada_body,
        grid=(n_layers, n // tn),
        in_specs=[pl.BlockSpec((rows, d), lambda l, j: (0, 0)),
                  pl.BlockSpec((1, d, tn), lambda l, j: (l, 0, j)),
                  pl.BlockSpec((1, 1, tn), lambda l, j: (l, 0, j))],
        out_specs=pl.BlockSpec((1, rows, tn), lambda l, j: (l, 0, j)),
        out_shape=jax.ShapeDtypeStruct((n_layers, rows, n), F32),
        compiler_params=_cparams("parallel", "parallel"),
    )(cc, w_ada, b_ada.reshape(n_layers, 1, n))


def _rope(xc, cos, sin, half, lane):
    fwd = pltpu.roll(xc, LANES - half, 1)
    bwd = pltpu.roll(xc, half, 1)
    sw = jnp.where((lane % (2 * half)) < half, fwd, bwd)
    return xc * cos + sw * sin


def _inproj_body(x_ref, sc_ref, sh_ref, g_ref, w_ref, qg_ref, kg_ref, bd_ref,
                 cg_ref, sg_ref, cd_ref, sd_ref, hy_ref, q_ref, k_ref, vt_ref):
    tm = x_ref.shape[1]
    h = _rms(x_ref[0]) * g_ref[...]
    h = h * (1.0 + sc_ref[0]) + sh_ref[0]
    p = jnp.dot(h.astype(BF16), w_ref[...], preferred_element_type=F32)
    hy_ref[0] = p[:, :HY_COLS]

    lane = lax.broadcasted_iota(jnp.int32, (tm, LANES), 1)
    cg, sg, cd, sd = cg_ref[...], sg_ref[...], cd_ref[...], sd_ref[...]
    bd = bd_ref[...]

    def head_norm(xc, gain):
        ms = jnp.dot(xc * xc, bd, precision=HIGHEST, preferred_element_type=F32) * (1.0 / HEAD_DIM)
        return xc * lax.rsqrt(ms + NORM_EPS) * gain

    for c in range(GQA_Q_COLS // LANES):
        xc = p[:, Q_START + c * LANES: Q_START + (c + 1) * LANES]
        xc = head_norm(xc, qg_ref[...])
        q_ref[0, :, c * LANES:(c + 1) * LANES] = _rope(xc, cg, sg, HEAD_DIM // 4, lane).astype(BF16)
    dscale = DIFF_QK_DIM ** -0.5 * LOG2E
    for c in range(DIFF_Q_COLS // LANES):
        o = Q_START + GQA_Q_COLS + c * LANES
        xc = p[:, o:o + LANES] * dscale
        q_ref[0, :, GQA_Q_COLS + c * LANES: GQA_Q_COLS + (c + 1) * LANES] = (
            _rope(xc, cd, sd, DIFF_QK_DIM // 4, lane).astype(BF16))

    gk = head_norm(p[:, K_START:K_START + GQA_KV_COLS], kg_ref[...])
    k_ref[0, :, :GQA_KV_COLS] = _rope(gk, cg, sg, HEAD_DIM // 4, lane).astype(BF16)
    dk0 = K_START + 2 * GQA_KV_COLS
    for c in range(DIFF_Q_COLS // LANES):
        xc = p[:, dk0 + c * LANES: dk0 + (c + 1) * LANES]
        k_ref[0, :, GQA_KV_COLS + c * LANES: GQA_KV_COLS + (c + 1) * LANES] = (
            _rope(xc, cd, sd, DIFF_QK_DIM // 4, lane).astype(BF16))
    kb = vt_ref.shape[3]
    for sblk in range(tm // kb):
        rows = slice(sblk * kb, (sblk + 1) * kb)
        vals = jnp.concatenate([p[rows, K_START + GQA_KV_COLS:K_START + 2 * GQA_KV_COLS],
                                p[rows, dk0 + DIFF_Q_COLS:]], axis=1)
        vals_t = vals.T.astype(BF16)
        ones = jnp.ones((V_HEAD_ROWS - HEAD_DIM, kb), BF16)
        for hd in range(KV_COLS // HEAD_DIM):
            vt_ref[0, sblk, hd * V_HEAD_ROWS:hd * V_HEAD_ROWS + HEAD_DIM, :] = vals_t[hd * HEAD_DIM:(hd + 1) * HEAD_DIM]
            vt_ref[0, sblk, hd * V_HEAD_ROWS + HEAD_DIM:(hd + 1) * V_HEAD_ROWS, :] = ones


def _inproj(x, scale, shift, g, w, qg, kg, bd, tables, tm, kb):
    b, t, d = x.shape
    row = lambda bi, i: (bi, i, 0)
    per_b = lambda bi, i: (bi, 0, 0)
    const = lambda bi, i: (0, 0)
    tab = pl.BlockSpec((tm, LANES), lambda bi, i: (i, 0))
    return pl.pallas_call(
        _inproj_body,
        grid=(b, t // tm),
        in_specs=[pl.BlockSpec((1, tm, d), row),
                  pl.BlockSpec((1, 1, d), per_b), pl.BlockSpec((1, 1, d), per_b),
                  pl.BlockSpec((1, d), const),
                  pl.BlockSpec((d, IN_COLS), const),
                  pl.BlockSpec((1, LANES), const), pl.BlockSpec((1, LANES), const),
                  pl.BlockSpec((LANES, LANES), const),
                  tab, tab, tab, tab],
        out_specs=[pl.BlockSpec((1, tm, HY_COLS), row), pl.BlockSpec((1, tm, Q_COLS), row),
                   pl.BlockSpec((1, tm, KV_COLS), row),
                   pl.BlockSpec((1, tm // kb, VT_ROWS, kb), lambda bi, i: (bi, i, 0, 0))],
        out_shape=[jax.ShapeDtypeStruct((b, t, HY_COLS), F32), jax.ShapeDtypeStruct((b, t, Q_COLS), BF16),
                   jax.ShapeDtypeStruct((b, t, KV_COLS), BF16),
                   jax.ShapeDtypeStruct((b, t // kb, VT_ROWS, kb), BF16)],
        compiler_params=_cparams("parallel", "parallel"),
    )(x, scale, shift, g, w, qg, kg, bd, *tables)


def _filter_body(f_ref, w1_ref, b1_ref, w2_ref, b2_ref, w3_ref, fr_ref, dc_ref, o_ref):
    n = f_ref.shape[0]
    fr = fr_ref[...]
    h = jnp.sin(fr * (jnp.dot(f_ref[...], w1_ref[...], precision=HIGHEST, preferred_element_type=F32) + b1_ref[...]))
    h = jnp.sin(fr * (jnp.dot(h, w2_ref[...], precision=HIGHEST, preferred_element_type=F32) + b2_ref[...]))
    k = jnp.dot(h, w3_ref[...], precision=HIGHEST, preferred_element_type=F32)
    t = lax.broadcasted_iota(jnp.int32, k.shape, 0)
    off = jnp.abs(t - n // 2).astype(F32) * (2.0 / n)
    k = k * jnp.exp(-off * jnp.abs(dc_ref[...]))
    k = k / jnp.sum(jnp.abs(k), axis=0, keepdims=True)
    o_ref[...] = k.astype(BF16)


def _hyena_filters(feats, w1p, b1, w2, b2, w3, freq, decay):
    n = feats.shape[0]
    hid = w2.shape[0]
    cols = w3.shape[1]
    const = lambda j: (0, 0)
    return pl.pallas_call(
        _filter_body,
        grid=(cols // LANES,),
        in_specs=[pl.BlockSpec((n, LANES), const), pl.BlockSpec((LANES, hid), const),
                  pl.BlockSpec((1, hid), const), pl.BlockSpec((hid, hid), const), pl.BlockSpec((1, hid), const),
                  pl.BlockSpec((hid, LANES), lambda j: (0, j)), pl.BlockSpec((1, hid), const),
                  pl.BlockSpec((1, LANES), lambda j: (0, j))],
        out_specs=pl.BlockSpec((n, LANES), lambda j: (0, j)),
        out_shape=jax.ShapeDtypeStruct((n, cols), BF16),
        compiler_params=_cparams("parallel"),
    )(feats, w1p, b1, w2, b2, w3, freq, decay)


def _hyprep_body(p_ref, prev_ref, next_ref, w_ref, b_ref, v_ref, vb_ref, x1_ref, x2_ref, buf):
    i = pl.program_id(1)
    n = pl.num_programs(1)
    tm = p_ref.shape[1]
    buf[SUBLANES:tm + SUBLANES, :] = p_ref[0]
    buf[SUBLANES - 1:SUBLANES, :] = jnp.where(i > 0, prev_ref[0][SUBLANES - 1:SUBLANES, :], 0.0)
    buf[tm + SUBLANES:tm + SUBLANES + 1, :] = jnp.where(i < n - 1, next_ref[0][0:1, :], 0.0)
    w = w_ref[...]
    y = (buf[SUBLANES - 1:tm + SUBLANES - 1, :] * w[0:1] + buf[SUBLANES:tm + SUBLANES, :] * w[1:2]
         + buf[SUBLANES + 1:tm + SUBLANES + 1, :] * w[2:3] + b_ref[...])
    v_ref[...] = y[:, :HY_WIDTH]
    vb_ref[...] = y[:, :HY_WIDTH].astype(BF16)
    x1_ref[...] = y[:, HY_WIDTH:2 * HY_WIDTH]
    x2_ref[...] = y[:, 2 * HY_WIDTH:]


def _hyprep(p_hy, conv_w, conv_b, tm):
    b, t, _ = p_hy.shape
    nb8 = t // SUBLANES
    r8 = tm // SUBLANES
    tmaj = pl.BlockSpec((tm, HY_WIDTH), lambda bi, i: (i, bi))
    shp = lambda dt: jax.ShapeDtypeStruct((t, b * HY_WIDTH), dt)
    return pl.pallas_call(
        _hyprep_body,
        grid=(b, t // tm),
        in_specs=[pl.BlockSpec((1, tm, HY_COLS), lambda bi, i: (bi, i, 0)),
                  pl.BlockSpec((1, SUBLANES, HY_COLS), lambda bi, i: (bi, jnp.maximum(i * r8 - 1, 0), 0)),
                  pl.BlockSpec((1, SUBLANES, HY_COLS), lambda bi, i: (bi, jnp.minimum((i + 1) * r8, nb8 - 1), 0)),
                  pl.BlockSpec((3, HY_COLS), lambda bi, i: (0, 0)),
                  pl.BlockSpec((1, HY_COLS), lambda bi, i: (0, 0))],
        out_specs=[tmaj, tmaj, tmaj, tmaj],
        out_shape=[shp(F32), shp(BF16), shp(F32), shp(F32)],
        scratch_shapes=[pltpu.VMEM((tm + 2 * SUBLANES, HY_COLS), F32)],
        compiler_params=_cparams("parallel", "parallel"),
    )(p_hy, p_hy, p_hy, conv_w, conv_b)


def _dft_mats(n_t):
    n_fft = 3 * n_t // 2
    half = n_fft // 2
    t = jnp.arange(n_t, dtype=jnp.int32)
    f = jnp.arange(half, dtype=jnp.int32)
    step = 2.0 * math.pi / n_fft
    ang = ((f[:, None] * t[None, :]) % n_fft).astype(F32) * step
    alt_t = (1 - 2 * (t % 2)).astype(F32)
    f_re = jnp.cos(ang)
    f_im = jnp.where(f[:, None] == 0, alt_t[None, :], -jnp.sin(ang))
    fwd = jnp.concatenate([f_re, f_im], axis=0).astype(BF16)
    n_out = t + n_t // 2
    ang2 = ((n_out[:, None] * f[None, :]) % n_fft).astype(F32) * step
    alt_n = (1 - 2 * (n_out % 2)).astype(F32)
    g_re = jnp.where(f[None, :] == 0, 1.0, 2.0 * jnp.cos(ang2)).astype(BF16)
    g_im = jnp.where(f[None, :] == 0, alt_n[:, None], -2.0 * jnp.sin(ang2)).astype(BF16)
    return fwd, g_re, g_im


def _dft_fwd_body(fre_ref, fim_ref, u_ref, zre_ref, zim_ref):
    u = u_ref[...]
    zre_ref[...] = jnp.dot(fre_ref[...], u, preferred_element_type=F32)
    zim_ref[...] = jnp.dot(fim_ref[...], u, preferred_element_type=F32)


def _dft_fwd_mul_body(fre_ref, fim_ref, u_ref, kre_ref, kim_ref, zre_ref, zim_ref):
    u = u_ref[...]
    xr = jnp.dot(fre_ref[...], u, preferred_element_type=F32)
    xi = jnp.dot(fim_ref[...], u, preferred_element_type=F32)
    kre, kim = kre_ref[...], kim_ref[...]
    row0 = (lax.broadcasted_iota(jnp.int32, xr.shape, 0) == 0) & (pl.program_id(0) == 0)
    a = xr * kre
    bb = xi * kim
    zre_ref[...] = (a - jnp.where(row0, 0.0, bb)).astype(zre_ref.dtype)
    zim_ref[...] = jnp.where(row0, bb, xr * kim + xi * kre).astype(zim_ref.dtype)


def _dft_fwd(fwd, u, tm, kf=None, order=0):
    n_fft, n_t = fwd.shape
    half = n_fft // 2
    cols = u.shape[1]
    nblk = half // tm
    tn = HY_WIDTH
    in_specs = [pl.BlockSpec((tm, n_t), lambda i, j: (i, 0)),
                pl.BlockSpec((tm, n_t), lambda i, j: (i + nblk, 0)),
                pl.BlockSpec((n_t, tn), lambda i, j: (0, j))]
    args = [fwd, fwd, u]
    if kf is None:
        body, odt = _dft_fwd_body, F32
    else:
        body, odt = _dft_fwd_mul_body, BF16
        in_specs += [pl.BlockSpec((tm, tn), lambda i, j: (i, order)), pl.BlockSpec((tm, tn), lambda i, j: (i, order))]
        args += [kf[0], kf[1]]
    out = pl.BlockSpec((tm, tn), lambda i, j: (i, j))
    return pl.pallas_call(
        body,
        grid=(nblk, cols // tn),
        in_specs=in_specs,
        out_specs=[out, out],
        out_shape=[jax.ShapeDtypeStruct((half, cols), odt)] * 2,
        compiler_params=_cparams("parallel", "parallel"),
    )(*args)


def _dft_inv_body(gre_ref, gim_ref, zre_ref, zim_ref, u_ref, x_ref, bias_ref, *out_refs, inv_n):
    y = (jnp.dot(gre_ref[...], zre_ref[...], preferred_element_type=F32)
         + jnp.dot(gim_ref[...], zim_ref[...], preferred_element_type=F32)) * inv_n
    o = x_ref[...] * (y + u_ref[...] * bias_ref[...])
    if len(out_refs) == 2:
        out_refs[0][...] = o
        out_refs[1][...] = o.astype(BF16)
    else:
        out_refs[0][0] = o.astype(BF16)


def _dft_inv(g_re, g_im, zre, zim, u, xg, bias, tm, batch, last):
    n_t, half = g_re.shape
    cols = u.shape[1]
    tn = HY_WIDTH
    gsp = pl.BlockSpec((tm, half), lambda i, j: (i, 0))
    zsp = pl.BlockSpec((half, tn), lambda i, j: (0, j))
    tsp = pl.BlockSpec((tm, tn), lambda i, j: (i, j))
    if last:
        out_specs = [pl.BlockSpec((1, tm, tn), lambda i, j: (j, i, 0))]
        out_shape = [jax.ShapeDtypeStruct((batch, n_t, tn), BF16)]
    else:
        out_specs = [tsp, tsp]
        out_shape = [jax.ShapeDtypeStruct((n_t, cols), F32), jax.ShapeDtypeStruct((n_t, cols), BF16)]
    return pl.pallas_call(
        functools.partial(_dft_inv_body, inv_n=1.0 / (2 * half)),
        grid=(n_t // tm, cols // tn),
        in_specs=[gsp, gsp, zsp, zsp, tsp, tsp, pl.BlockSpec((1, tn), lambda i, j: (0, 0))],
        out_specs=out_specs,
        out_shape=out_shape,
        compiler_params=_cparams("parallel", "parallel"),
    )(g_re, g_im, zre, zim, u, xg, bias)


def _hyena(p_hy, conv_w, conv_b, hy_bias, kf, mats, tm_prep, tm_f, tm_t):
    b = p_hy.shape[0]
    fwd, g_re, g_im = mats
    v, vb, x1, x2 = _hyprep(p_hy, conv_w, conv_b, tm_prep)
    zre, zim = _dft_fwd(fwd, vb, tm_f, kf, 0)
    y1, y1b = _dft_inv(g_re, g_im, zre, zim, v, x1, hy_bias[0:1], tm_t, b, False)
    zre, zim = _dft_fwd(fwd, y1b, tm_f, kf, 1)
    (hy,) = _dft_inv(g_re, g_im, zre, zim, y1, x2, hy_bias[1:2], tm_t, b, True)
    return hy


def _attn_body(q_ref, k_ref, vt_ref, lam_ref, sub_ref, go_ref, do_ref, *, lam_init, group):
    tq = q_ref.shape[1]
    nkb, _, kb = vt_ref.shape[1:]
    lp = lam_ref[...]
    lam = (jnp.exp(jnp.sum(lp[0:1] * lp[1:2], axis=1, keepdims=True))
           - jnp.exp(jnp.sum(lp[2:3] * lp[3:4], axis=1, keepdims=True)) + lam_init)

    def attend(maps):
        def score(i, qm, kcol, kw):
            off = pl.multiple_of(i * kb, kb)
            return lax.dot_general(k_ref[0, pl.ds(off, kb), kcol:kcol + kw], qm, (((1,), (1,)), ((), ())),
                                   preferred_element_type=F32)

        def update(i, s, m, acc, vrow):
            vblk = vt_ref[0, i, vrow:vrow + V_HEAD_ROWS, :]
            m_new = jnp.maximum(m, jnp.max(s, axis=0, keepdims=True))
            alpha = jnp.exp2(m - m_new)
            p = jnp.exp2((s - m_new).astype(BF16))
            return m_new, alpha * acc + jnp.dot(vblk, p, preferred_element_type=F32)

        def body(i, state):
            scores = [score(i, qm, kcol, kw) for (qm, kcol, kw, _) in maps]
            return tuple(update(i, s, m, acc, vrow) for (_, _, _, vrow), s, (m, acc) in zip(maps, scores, state))

        init = tuple((jnp.full((1, tq), NEG_BIG, F32), jnp.zeros((V_HEAD_ROWS, tq), F32)) for _ in maps)
        return [acc[:HEAD_DIM] / acc[HEAD_DIM:HEAD_DIM + 1] for (_, acc) in lax.fori_loop(0, nkb, body, init)]

    lane = lax.broadcasted_iota(jnp.int32, (tq, LANES), 1)
    lane_d = lax.broadcasted_iota(jnp.int32, (tq, DIFF_Q_COLS), 1)
    dq = q_ref[0, :, GQA_Q_COLS:]
    maps = []
    for c in range(GQA_Q_COLS // LANES):
        qc = q_ref[0, :, c * LANES:(c + 1) * LANES]
        for j in range(GQA_KV_HEADS):
            maps.append((jnp.where((lane // HEAD_DIM) == j, qc, jnp.zeros_like(qc)), 0, GQA_KV_COLS, j * V_HEAD_ROWS))
    for h in range(DIFF_HEADS):
        for cmap in range(2):
            qm = jnp.where((lane_d // DIFF_QK_DIM) == 2 * h + cmap, dq, jnp.zeros_like(dq))
            maps.append((qm, GQA_KV_COLS, DIFF_Q_COLS, (GQA_KV_HEADS + h) * V_HEAD_ROWS))
    outs = []
    for g in range(0, len(maps), group):
        outs += attend(maps[g:g + group])

    n_gqa = GQA_HEADS
    for c in range(GQA_Q_COLS // LANES):
        pair = jnp.concatenate(outs[GQA_KV_HEADS * c:GQA_KV_HEADS * (c + 1)], axis=0)
        go_ref[0, :, c * LANES:(c + 1) * LANES] = pair.T.astype(BF16)
    sub = sub_ref[...] * (1.0 - lam_init)
    heads = []
    for h in range(DIFF_HEADS):
        d = outs[n_gqa + 2 * h] - lam * outs[n_gqa + 2 * h + 1]
        ms = jnp.mean(d * d, axis=0, keepdims=True)
        heads.append(d * lax.rsqrt(ms + NORM_EPS) * sub)
    do_ref[0] = jnp.concatenate(heads, axis=0).T.astype(BF16)


def _attention(q, k, vt, lam_p, sub, lam_init, tq, group=8):
    b, t, _ = q.shape
    lk = k.shape[1]
    row = lambda bi, i: (bi, i, 0)
    sub_t = jnp.broadcast_to(sub.reshape(DIFF_V_DIM, 1), (DIFF_V_DIM, tq))
    return pl.pallas_call(
        functools.partial(_attn_body, lam_init=lam_init, group=group),
        grid=(b, t // tq),
        in_specs=[pl.BlockSpec((1, tq, Q_COLS), row),
                  pl.BlockSpec((1, lk, KV_COLS), lambda bi, i: (bi, 0, 0)),
                  pl.BlockSpec((1,) + vt.shape[1:], lambda bi, i: (bi, 0, 0, 0)),
                  pl.BlockSpec(lam_p.shape, lambda bi, i: (0, 0)),
                  pl.BlockSpec((DIFF_V_DIM, tq), lambda bi, i: (0, 0))],
        out_specs=[pl.BlockSpec((1, tq, GQA_Q_COLS), row), pl.BlockSpec((1, tq, DIFF_V_COLS), row)],
        out_shape=[jax.ShapeDtypeStruct((b, t, GQA_Q_COLS), BF16), jax.ShapeDtypeStruct((b, t, DIFF_V_COLS), BF16)],
        compiler_params=_cparams("parallel", "parallel"),
    )(q, k, vt, lam_p, sub_t)


def _outproj_body(hy_ref, go_ref, do_ref, w_ref, x_ref, g1_ref, gpost_ref, gpre_ref, sc_ref, sh_ref,
                  wr_ref, br_ref, xm_ref, h2_ref, comb_ref):
    y = (jnp.dot(hy_ref[0], w_ref[:HY_WIDTH], preferred_element_type=F32)
         + jnp.dot(go_ref[0], w_ref[HY_WIDTH:HY_WIDTH + GQA_Q_COLS], preferred_element_type=F32)
         + jnp.dot(do_ref[0], w_ref[HY_WIDTH + GQA_Q_COLS:], preferred_element_type=F32))
    xm = x_ref[0] + g1_ref[0] * (_rms(y) * gpost_ref[...])
    xm_ref[0] = xm
    h2 = _rms(xm) * gpre_ref[...] * (1.0 + sc_ref[0]) + sh_ref[0]
    h2_ref[0] = h2.astype(BF16)

    logits = jnp.dot(h2, wr_ref[...], precision=HIGHEST, preferred_element_type=F32) + br_ref[...]
    lane = lax.broadcasted_iota(jnp.int32, logits.shape, 1)
    comb = jnp.zeros_like(logits)
    denom = jnp.zeros((logits.shape[0], 1), F32)
    top = None
    for _ in range(TOP_K):
        mx = jnp.max(logits, axis=1, keepdims=True)
        first = jnp.min(jnp.where(logits == mx, lane, LANES), axis=1, keepdims=True)
        sel = lane == first
        top = mx if top is None else top
        e = jnp.exp(mx - top)
        comb = jnp.where(sel, e, comb)
        denom = denom + e
        logits = jnp.where(sel, NEG_BIG, logits)
    comb_ref[0] = comb / denom


def _outproj(hy, go, do, w_out, x, g1, gpost, gpre, sc2, sh2, wr, br, tm):
    b, t, d = x.shape
    row = lambda bi, i: (bi, i, 0)
    per_b = lambda bi, i: (bi, 0, 0)
    const = lambda bi, i: (0, 0)
    vec = pl.BlockSpec((1, 1, d), per_b)
    return pl.pallas_call(
        _outproj_body,
        grid=(b, t // tm),
        in_specs=[pl.BlockSpec((1, tm, HY_WIDTH), row), pl.BlockSpec((1, tm, GQA_Q_COLS), row),
                  pl.BlockSpec((1, tm, DIFF_V_COLS), row), pl.BlockSpec(w_out.shape, const),
                  pl.BlockSpec((1, tm, d), row), vec, pl.BlockSpec((1, d), const), pl.BlockSpec((1, d), const),
                  vec, vec, pl.BlockSpec((d, LANES), const), pl.BlockSpec((1, LANES), const)],
        out_specs=[pl.BlockSpec((1, tm, d), row), pl.BlockSpec((1, tm, d), row), pl.BlockSpec((1, tm, LANES), row)],
        out_shape=[jax.ShapeDtypeStruct((b, t, d), F32), jax.ShapeDtypeStruct((b, t, d), BF16),
                   jax.ShapeDtypeStruct((b, t, LANES), F32)],
        compiler_params=_cparams("parallel", "parallel"),
    )(hy, go, do, w_out, x, g1, gpost, gpre, sc2, sh2, wr, br)


def _moe_body(h_ref, comb_ref, wgu_ref, bgu_ref, wd_ref, bd_ref, x_ref, g2_ref, gpost_ref, o_ref, acc_ref):
    e = pl.program_id(1)
    ff = wd_ref.shape[1]

    @pl.when(e == 0)
    def _():
        acc_ref[...] = jnp.zeros_like(acc_ref)

    gu = jnp.dot(h_ref[...], wgu_ref[0], preferred_element_type=F32) + bgu_ref[0]
    gate = jnp.minimum(gu[:, :ff], SWIGLU_LIMIT)
    up = jnp.clip(gu[:, ff:], -SWIGLU_LIMIT, SWIGLU_LIMIT)
    act = (up + 1.0) * gate / (1.0 + jnp.exp(-SWIGLU_ALPHA * gate))
    y = jnp.dot(act.astype(BF16), wd_ref[0], preferred_element_type=F32) + bd_ref[0]
    comb = comb_ref[...]
    lane = lax.broadcasted_iota(jnp.int32, comb.shape, 1)
    cw = jnp.sum(jnp.where(lane == e, comb, 0.0), axis=1, keepdims=True)
    acc_ref[...] += cw * y

    @pl.when(e == pl.num_programs(1) - 1)
    def _():
        o_ref[...] = x_ref[...] + g2_ref[0] * (_rms(acc_ref[...]) * gpost_ref[...])


def _moe(h2, comb, wgu, bgu, wd, bd, xm, g2, gpost, tm):
    b, t, d = xm.shape
    n_exp, _, ff2 = wgu.shape
    ff = ff2 // 2
    n = b * t
    per_tile = t // tm
    tok = lambda i, e: (i, 0)
    out = pl.pallas_call(
        _moe_body,
        grid=(n // tm, n_exp),
        in_specs=[pl.BlockSpec((tm, d), tok), pl.BlockSpec((tm, LANES), tok),
                  pl.BlockSpec((1, d, ff2), lambda i, e: (e, 0, 0)), pl.BlockSpec((1, 1, ff2), lambda i, e: (e, 0, 0)),
                  pl.BlockSpec((1, ff, d), lambda i, e: (e, 0, 0)), pl.BlockSpec((1, 1, d), lambda i, e: (e, 0, 0)),
                  pl.BlockSpec((tm, d), tok), pl.BlockSpec((1, 1, d), lambda i, e: (i // per_tile, 0, 0)),
                  pl.BlockSpec((1, d), lambda i, e: (0, 0))],
        out_specs=pl.BlockSpec((tm, d), tok),
        out_shape=jax.ShapeDtypeStruct((n, d), F32),
        scratch_shapes=[pltpu.VMEM((tm, d), F32)],
        compiler_params=_cparams("parallel", "arbitrary"),
    )(h2.reshape(n, d), comb.reshape(n, LANES), wgu, bgu.reshape(n_exp, 1, ff2), wd, bd.reshape(n_exp, 1, d),
      xm.reshape(n, d), g2, gpost)
    return out.reshape(b, t, d)


def _rope_tables(n_tok, dim, use_pos):
    if not use_pos:
        return jnp.ones((n_tok, LANES), F32), jnp.zeros((n_tok, LANES), F32)
    half = dim // 4
    t = jnp.arange(n_tok, dtype=jnp.int32)
    row = (t // GRID_W).astype(F32)
    col = (t % GRID_W).astype(F32)
    inv_freq = ROPE_THETA ** (-jnp.arange(half, dtype=F32) / half)
    a_row = row[:, None] * inv_freq[None, :]
    a_col = col[:, None] * inv_freq[None, :]
    cos = jnp.concatenate([jnp.cos(a_row)] * 2 + [jnp.cos(a_col)] * 2, axis=1)
    sin = jnp.concatenate([-jnp.sin(a_row), jnp.sin(a_row), -jnp.sin(a_col), jnp.sin(a_col)], axis=1)
    reps = LANES // dim
    return jnp.tile(cos, (1, reps)), jnp.tile(sin, (1, reps))


def _hyena_feats(n_t):
    t = jnp.arange(n_t, dtype=F32)
    bands = jnp.linspace(1e-4, HY_BANDS - 1, HY_BANDS, dtype=F32)
    w = (2.0 * math.pi / n_t) * t
    feats = jnp.concatenate([(t / n_t)[:, None], jnp.cos(w[:, None] * bands), -jnp.sin(w[:, None] * bands)], axis=-1)
    return jnp.pad(feats, ((0, 0), (0, LANES - feats.shape[1])))


def _tile_rows(t, cap):
    return cap if t % cap == 0 else t


def kernel(x, c, ctx, c_ctx, w_ada, b_ada, g_pre_mix, g_post_mix, g_pre_ffn, g_post_ffn, w_in, w_out, hy_conv_w,
           hy_conv_b, hy_w1, hy_b1, hy_w2, hy_b2, hy_w3, hy_freq, hy_decay, hy_bias, q_norm, k_norm, lam_q1, lam_k1,
           lam_q2, lam_k2, subln, w_router, b_router, w_gu, b_gu, w_down, b_down):
    b, s, d = x.shape
    n_ctx = ctx.shape[1]
    depth = w_ada.shape[0]
    n_exp = w_router.shape[-1]

    rows = -(-(b + 1) // SUBLANES) * SUBLANES
    cc = jnp.zeros((rows, d), F32).at[:b].set(c).at[b].set(c_ctx)
    mod = _ada(cc, w_ada, b_ada)

    head_cols = jnp.concatenate([jnp.arange(HEAD_DIM) + HEAD_DIM * h for h in _GQA_HEAD_ORDER])
    in_perm = jnp.concatenate([jnp.arange(Q_START), Q_START + head_cols, jnp.arange(Q_START + GQA_Q_COLS, IN_COLS)])
    out_perm = jnp.concatenate([jnp.arange(HY_WIDTH), HY_WIDTH + head_cols, jnp.arange(HY_WIDTH + GQA_Q_COLS, d)])
    w_in_b = w_in[:, :, in_perm].astype(BF16)
    w_out_b = w_out[:, out_perm, :].astype(BF16)
    w_gu_b = w_gu.astype(BF16)
    w_down_b = w_down.astype(BF16)
    wr_p = jnp.pad(w_router, ((0, 0), (0, 0), (0, LANES - n_exp)))
    br_p = jnp.pad(b_router, ((0, 0), (0, LANES - n_exp)), constant_values=NEG_BIG)[:, None, :]

    lane = jnp.arange(LANES)
    block_diag = (lane[:, None] // HEAD_DIM == lane[None, :] // HEAD_DIM).astype(F32)
    reps = LANES // HEAD_DIM

    tabs_lat = _rope_tables(s, HEAD_DIM, True) + _rope_tables(s, DIFF_QK_DIM, True)
    tabs_ctx = _rope_tables(n_ctx, HEAD_DIM, False) + _rope_tables(n_ctx, DIFF_QK_DIM, False)
    mats_lat = _dft_mats(s)
    feats_lat = _hyena_feats(s)
    if depth > 1:
        mats_ctx = _dft_mats(n_ctx)
        feats_ctx = _hyena_feats(n_ctx)

    tm_lat = _tile_rows(s, 512)
    tm_ctx = _tile_rows(n_ctx, 256)
    lk = n_ctx + s
    kb = 256 if (tm_lat % 256 == 0 and tm_ctx % 256 == 0) else 128

    def filters(i, feats, mats):
        w1p = jnp.pad(hy_w1[i], ((0, LANES - hy_w1.shape[1]), (0, 0)))
        k = _hyena_filters(feats, w1p, hy_b1[i][None], hy_w2[i], hy_b2[i][None], hy_w3[i], hy_freq[i][None],
                           hy_decay[i][None])
        half = mats[0].shape[0] // 2
        return _dft_fwd(mats[0], k, _tile_rows(half, 512))

    xc = ctx
    for i in range(depth):
        ctx_out = i < depth - 1
        lam_init = 0.8 - 0.6 * math.exp(-0.3 * i)
        m_lat = mod[i, :b].reshape(b, 1, 6 * d)
        sh1, sc1, g1, sh2, sc2, g2 = [m_lat[:, :, j * d:(j + 1) * d] for j in range(6)]
        m_ctx = jnp.broadcast_to(mod[i, b].reshape(1, 1, 6 * d), (b, 1, 6 * d))
        csh1, csc1, cg1, csh2, csc2, cg2 = [m_ctx[:, :, j * d:(j + 1) * d] for j in range(6)]

        qg = jnp.tile(q_norm[i], reps)[None] * (HEAD_DIM ** -0.5 * LOG2E)
        kg = jnp.tile(k_norm[i], reps)[None]
        lam_p = jnp.stack([lam_q1[i], lam_k1[i], lam_q2[i], lam_k2[i]])
        proj = functools.partial(_inproj, g=g_pre_mix[i][None], w=w_in_b[i], qg=qg, kg=kg, bd=block_diag, kb=kb)

        hy_c, q_c, k_c, v_c = proj(xc, csc1, csh1, tables=tabs_ctx, tm=tm_ctx)
        hy_l, q_l, k_l, v_l = proj(x, sc1, sh1, tables=tabs_lat, tm=tm_lat)
        k_all = jnp.concatenate([k_c, k_l], axis=1)
        v_all = jnp.concatenate([v_c, v_l], axis=1)

        attn = functools.partial(_attention, lam_p=lam_p, sub=subln[i], lam_init=lam_init)
        go_l, do_l = attn(q_l, k_all, v_all, tq=_tile_rows(s, 256))
        kf_lat = filters(i, feats_lat, mats_lat)
        half_lat = mats_lat[0].shape[0] // 2
        hyo_l = _hyena(hy_l, hy_conv_w[i], hy_conv_b[i][None], hy_bias[i], kf_lat, mats_lat,
                       tm_lat, _tile_rows(half_lat, 512), tm_lat)

        post = functools.partial(_outproj, w_out=w_out_b[i], gpost=g_post_mix[i][None], gpre=g_pre_ffn[i][None],
                                 wr=wr_p[i], br=br_p[i])
        moe = functools.partial(_moe, wgu=w_gu_b[i], bgu=b_gu[i], wd=w_down_b[i], bd=b_down[i],
                                gpost=g_post_ffn[i][None])

        xm, h2, comb = post(hyo_l, go_l, do_l, x=x, g1=g1, sc2=sc2, sh2=sh2, tm=tm_lat)
        x = moe(h2, comb, xm=xm, g2=g2, tm=_tile_rows(s, 512))

        if ctx_out:
            go_c, do_c = attn(q_c, k_c, v_c, tq=tm_ctx)
            kf_ctx = filters(i, feats_ctx, mats_ctx)
            half_ctx = mats_ctx[0].shape[0] // 2
            hyo_c = _hyena(hy_c, hy_conv_w[i], hy_conv_b[i][None], hy_bias[i], kf_ctx, mats_ctx,
                           tm_ctx, _tile_rows(half_ctx, 512), tm_ctx)
            xcm, h2c, combc = post(hyo_c, go_c, do_c, x=xc, g1=cg1, sc2=csc2, sh2=csh2, tm=tm_ctx)
            xc = moe(h2c, combc, xm=xcm, g2=cg2, tm=tm_ctx)
    return x
```

```python
import functools
import math

import jax
import jax.numpy as jnp
from jax import lax
from jax.experimental import pallas as pl
from jax.experimental.pallas import tpu as pltpu

F32 = jnp.float32
BF16 = jnp.bfloat16
HIGHEST = lax.Precision.HIGHEST

HEAD_DIM = 64
ROPE_THETA = 10000.0
NORM_EPS = 1e-6
GRID_W = 64
HY_WIDTH = 256
HY_BANDS = 16
GQA_HEADS = 8
GQA_KV_HEADS = 2
GQA_GROUP = GQA_HEADS // GQA_KV_HEADS
DIFF_HEADS = 4
DIFF_QK_DIM = 32
DIFF_V_DIM = 64
TOP_K = 4
SWIGLU_LIMIT = 7.0
SWIGLU_ALPHA = 1.702

HY_COLS = 3 * HY_WIDTH
GQA_Q_COLS = GQA_HEADS * HEAD_DIM
DIFF_Q_COLS = DIFF_HEADS * 2 * DIFF_QK_DIM
GQA_KV_COLS = GQA_KV_HEADS * HEAD_DIM
DIFF_V_COLS = DIFF_HEADS * DIFF_V_DIM
Q_COLS = GQA_Q_COLS + DIFF_Q_COLS
KV_COLS = GQA_KV_COLS + DIFF_Q_COLS
Q_START = HY_COLS
K_START = Q_START + Q_COLS
IN_COLS = K_START + 2 * KV_COLS
V_HEAD_ROWS = HEAD_DIM + 16
VT_ROWS = (KV_COLS // HEAD_DIM) * V_HEAD_ROWS

LANES = 128
SUBLANES = 8
VMEM_LIMIT_BYTES = 56 * 1024 * 1024
NEG_BIG = -1e30
LOG2E = math.log2(math.e)

_GQA_HEAD_ORDER = tuple(g + GQA_GROUP * j for g in range(GQA_GROUP) for j in range(GQA_KV_HEADS))


def _cparams(*sem):
    return pltpu.CompilerParams(dimension_semantics=sem, vmem_limit_bytes=VMEM_LIMIT_BYTES)


def _rms(x):
    return x * lax.rsqrt(jnp.mean(x * x, axis=-1, keepdims=True) + NORM_EPS)


def _ada_body(c_ref, w_ref, b_ref, o_ref):
    c = c_ref[...]
    s = c / (1.0 + jnp.exp(-c))
    o_ref[0] = jnp.dot(s.astype(BF16), w_ref[0].astype(BF16), preferred_element_type=F32) + b_ref[0]


def _ada(cc, w_ada, b_ada):
    n_layers, d, n = w_ada.shape
    rows = cc.shape[0]
    tn = 512
    return pl.pallas_call(
        _ada_body,
        grid=(n_layers, n // tn),
        in_specs=[pl.BlockSpec((rows, d), lambda l, j: (0, 0)),
                  pl.BlockSpec((1, d, tn), lambda l, j: (l, 0, j)),
                  pl.BlockSpec((1, 1, tn), lambda l, j: (l, 0, j))],
        out_specs=pl.BlockSpec((1, rows, tn), lambda l, j: (l, 0, j)),
        out_shape=jax.ShapeDtypeStruct((n_layers, rows, n), F32),
        compiler_params=_cparams("parallel", "parallel"),
    )(cc, w_ada, b_ada.reshape(n_layers, 1, n))


def _rope(xc, cos, sin, half, lane):
    fwd = pltpu.roll(xc, LANES - half, 1)
    bwd = pltpu.roll(xc, half, 1)
    sw = jnp.where((lane % (2 * half)) < half, fwd, bwd)
    return xc * cos + sw * sin


def _inproj_body(x_ref, sc_ref, sh_ref, g_ref, w_ref, qg_ref, kg_ref, bd_ref,
                 cg_ref, sg_ref, cd_ref, sd_ref, hy_ref, q_ref, k_ref, vt_ref):
    tm = x_ref.shape[1]
    h = _rms(x_ref[0]) * g_ref[...]
    h = h * (1.0 + sc_ref[0]) + sh_ref[0]
    p = jnp.dot(h.astype(BF16), w_ref[...], preferred_element_type=F32)
    hy_ref[0] = p[:, :HY_COLS]

    lane = lax.broadcasted_iota(jnp.int32, (tm, LANES), 1)
    cg, sg, cd, sd = cg_ref[...], sg_ref[...], cd_ref[...], sd_ref[...]
    bd = bd_ref[...]

    def head_norm(xc, gain):
        ms = jnp.dot(xc * xc, bd, precision=HIGHEST, preferred_element_type=F32) * (1.0 / HEAD_DIM)
        return xc * lax.rsqrt(ms + NORM_EPS) * gain

    for c in range(GQA_Q_COLS // LANES):
        xc = p[:, Q_START + c * LANES: Q_START + (c + 1) * LANES]
        xc = head_norm(xc, qg_ref[...])
        q_ref[0, :, c * LANES:(c + 1) * LANES] = _rope(xc, cg, sg, HEAD_DIM // 4, lane).astype(BF16)
    dscale = DIFF_QK_DIM ** -0.5 * LOG2E
    for c in range(DIFF_Q_COLS // LANES):
        o = Q_START + GQA_Q_COLS + c * LANES
        xc = p[:, o:o + LANES] * dscale
        q_ref[0, :, GQA_Q_COLS + c * LANES: GQA_Q_COLS + (c + 1) * LANES] = (
            _rope(xc, cd, sd, DIFF_QK_DIM // 4, lane).astype(BF16))

    gk = head_norm(p[:, K_START:K_START + GQA_KV_COLS], kg_ref[...])
    k_ref[0, :, :GQA_KV_COLS] = _rope(gk, cg, sg, HEAD_DIM // 4, lane).astype(BF16)
    dk0 = K_START + 2 * GQA_KV_COLS
    for c in range(DIFF_Q_COLS // LANES):
        xc = p[:, dk0 + c * LANES: dk0 + (c + 1) * LANES]
        k_ref[0, :, GQA_KV_COLS + c * LANES: GQA_KV_COLS + (c + 1) * LANES] = (
            _rope(xc, cd, sd, DIFF_QK_DIM // 4, lane).astype(BF16))
    kb = vt_ref.shape[3]
    for sblk in range(tm // kb):
        rows = slice(sblk * kb, (sblk + 1) * kb)
        vals = jnp.concatenate([p[rows, K_START + GQA_KV_COLS:K_START + 2 * GQA_KV_COLS],
                                p[rows, dk0 + DIFF_Q_COLS:]], axis=1)
        vals_t = vals.T.astype(BF16)
        ones = jnp.ones((V_HEAD_ROWS - HEAD_DIM, kb), BF16)
        for hd in range(KV_COLS // HEAD_DIM):
            vt_ref[0, sblk, hd * V_HEAD_ROWS:hd * V_HEAD_ROWS + HEAD_DIM, :] = vals_t[hd * HEAD_DIM:(hd + 1) * HEAD_DIM]
            vt_ref[0, sblk, hd * V_HEAD_ROWS + HEAD_DIM:(hd + 1) * V_HEAD_ROWS, :] = ones


def _inproj(x, scale, shift, g, w, qg, kg, bd, tables, tm, kb):
    b, t, d = x.shape
    row = lambda bi, i: (bi, i, 0)
    per_b = lambda bi, i: (bi, 0, 0)
    const = lambda bi, i: (0, 0)
    tab = pl.BlockSpec((tm, LANES), lambda bi, i: (i, 0))
    return pl.pallas_call(
        _inproj_body,
        grid=(b, t // tm),
        in_specs=[pl.BlockSpec((1, tm, d), row),
                  pl.BlockSpec((1, 1, d), per_b), pl.BlockSpec((1, 1, d), per_b),
                  pl.BlockSpec((1, d), const),
                  pl.BlockSpec((d, IN_COLS), const),
                  pl.BlockSpec((1, LANES), const), pl.BlockSpec((1, LANES), const),
                  pl.BlockSpec((LANES, LANES), const),
                  tab, tab, tab, tab],
        out_specs=[pl.BlockSpec((1, tm, HY_COLS), row), pl.BlockSpec((1, tm, Q_COLS), row),
                   pl.BlockSpec((1, tm, KV_COLS), row),
                   pl.BlockSpec((1, tm // kb, VT_ROWS, kb), lambda bi, i: (bi, i, 0, 0))],
        out_shape=[jax.ShapeDtypeStruct((b, t, HY_COLS), F32), jax.ShapeDtypeStruct((b, t, Q_COLS), BF16),
                   jax.ShapeDtypeStruct((b, t, KV_COLS), BF16),
                   jax.ShapeDtypeStruct((b, t // kb, VT_ROWS, kb), BF16)],
        compiler_params=_cparams("parallel", "parallel"),
    )(x, scale, shift, g, w, qg, kg, bd, *tables)


def _filter_body(f_ref, w1_ref, b1_ref, w2_ref, b2_ref, w3_ref, fr_ref, dc_ref, o_ref):
    n = f_ref.shape[0]
    fr = fr_ref[...]
    h = jnp.sin(fr * (jnp.dot(f_ref[...], w1_ref[...], precision=HIGHEST, preferred_element_type=F32) + b1_ref[...]))
    h = jnp.sin(fr * (jnp.dot(h, w2_ref[...], precision=HIGHEST, preferred_element_type=F32) + b2_ref[...]))
    k = jnp.dot(h, w3_ref[...], precision=HIGHEST, preferred_element_type=F32)
    t = lax.broadcasted_iota(jnp.int32, k.shape, 0)
    off = jnp.abs(t - n // 2).astype(F32) * (2.0 / n)
    k = k * jnp.exp(-off * jnp.abs(dc_ref[...]))
    k = k / jnp.sum(jnp.abs(k), axis=0, keepdims=True)
    o_ref[...] = k.astype(BF16)


def _hyena_filters(feats, w1p, b1, w2, b2, w3, freq, decay):
    n = feats.shape[0]
    hid = w2.shape[0]
    cols = w3.shape[1]
    const = lambda j: (0, 0)
    return pl.pallas_call(
        _filter_body,
        grid=(cols // LANES,),
        in_specs=[pl.BlockSpec((n, LANES), const), pl.BlockSpec((LANES, hid), const),
                  pl.BlockSpec((1, hid), const), pl.BlockSpec((hid, hid), const), pl.BlockSpec((1, hid), const),
                  pl.BlockSpec((hid, LANES), lambda j: (0, j)), pl.BlockSpec((1, hid), const),
                  pl.BlockSpec((1, LANES), lambda j: (0, j))],
        out_specs=pl.BlockSpec((n, LANES), lambda j: (0, j)),
        out_shape=jax.ShapeDtypeStruct((n, cols), BF16),
        compiler_params=_cparams("parallel"),
    )(feats, w1p, b1, w2, b2, w3, freq, decay)


def _hyprep_body(p_ref, prev_ref, next_ref, w_ref, b_ref, v_ref, vb_ref, x1_ref, x2_ref, buf):
    i = pl.program_id(1)
    n = pl.num_programs(1)
    tm = p_ref.shape[1]
    buf[SUBLANES:tm + SUBLANES, :] = p_ref[0]
    buf[SUBLANES - 1:SUBLANES, :] = jnp.where(i > 0, prev_ref[0][SUBLANES - 1:SUBLANES, :], 0.0)
    buf[tm + SUBLANES:tm + SUBLANES + 1, :] = jnp.where(i < n - 1, next_ref[0][0:1, :], 0.0)
    w = w_ref[...]
    y = (buf[SUBLANES - 1:tm + SUBLANES - 1, :] * w[0:1] + buf[SUBLANES:tm + SUBLANES, :] * w[1:2]
         + buf[SUBLANES + 1:tm + SUBLANES + 1, :] * w[2:3] + b_ref[...])
    v_ref[...] = y[:, :HY_WIDTH]
    vb_ref[...] = y[:, :HY_WIDTH].astype(BF16)
    x1_ref[...] = y[:, HY_WIDTH:2 * HY_WIDTH]
    x2_ref[...] = y[:, 2 * HY_WIDTH:]


def _hyprep(p_hy, conv_w, conv_b, tm):
    b, t, _ = p_hy.shape
    nb8 = t // SUBLANES
    r8 = tm // SUBLANES
    tmaj = pl.BlockSpec((tm, HY_WIDTH), lambda bi, i: (i, bi))
    shp = lambda dt: jax.ShapeDtypeStruct((t, b * HY_WIDTH), dt)
    return pl.pallas_call(
        _hyprep_body,
        grid=(b, t // tm),
        in_specs=[pl.BlockSpec((1, tm, HY_COLS), lambda bi, i: (bi, i, 0)),
                  pl.BlockSpec((1, SUBLANES, HY_COLS), lambda bi, i: (bi, jnp.maximum(i * r8 - 1, 0), 0)),
                  pl.BlockSpec((1, SUBLANES, HY_COLS), lambda bi, i: (bi, jnp.minimum((i + 1) * r8, nb8 - 1), 0)),
                  pl.BlockSpec((3, HY_COLS), lambda bi, i: (0, 0)),
                  pl.BlockSpec((1, HY_COLS), lambda bi, i: (0, 0))],
        out_specs=[tmaj, tmaj, tmaj, tmaj],
        out_shape=[shp(F32), shp(BF16), shp(F32), shp(F32)],
        scratch_shapes=[pltpu.VMEM((tm + 2 * SUBLANES, HY_COLS), F32)],
        compiler_params=_cparams("parallel", "parallel"),
    )(p_hy, p_hy, p_hy, conv_w, conv_b)


def _dft_mats(n_t):
    n_fft = 3 * n_t // 2
    half = n_fft // 2
    t = jnp.arange(n_t, dtype=jnp.int32)
    f = jnp.arange(half, dtype=jnp.int32)
    step = 2.0 * math.pi / n_fft
    ang = ((f[:, None] * t[None, :]) % n_fft).astype(F32) * step
    alt_t = (1 - 2 * (t % 2)).astype(F32)
    f_re = jnp.cos(ang)
    f_im = jnp.where(f[:, None] == 0, alt_t[None, :], -jnp.sin(ang))
    fwd = jnp.concatenate([f_re, f_im], axis=0).astype(BF16)
    n_out = t + n_t // 2
    ang2 = ((n_out[:, None] * f[None, :]) % n_fft).astype(F32) * step
    alt_n = (1 - 2 * (n_out % 2)).astype(F32)
    g_re = jnp.where(f[None, :] == 0, 1.0, 2.0 * jnp.cos(ang2)).astype(BF16)
    g_im = jnp.where(f[None, :] == 0, alt_n[:, None], -2.0 * jnp.sin(ang2)).astype(BF16)
    return fwd, g_re, g_im


def _dft_fwd_body(fre_ref, fim_ref, u_ref, zre_ref, zim_ref):
    u = u_ref[...]
    zre_ref[...] = jnp.dot(fre_ref[...], u, preferred_element_type=F32)
    zim_ref[...] = jnp.dot(fim_ref[...], u, preferred_element_type=F32)


def _dft_fwd_mul_body(fre_ref, fim_ref, u_ref, kre_ref, kim_ref, zre_ref, zim_ref):
    u = u_ref[...]
    xr = jnp.dot(fre_ref[...], u, preferred_element_type=F32)
    xi = jnp.dot(fim_ref[...], u, preferred_element_type=F32)
    kre, kim = kre_ref[...], kim_ref[...]
    row0 = (lax.broadcasted_iota(jnp.int32, xr.shape, 0) == 0) & (pl.program_id(0) == 0)
    a = xr * kre
    bb = xi * kim
    zre_ref[...] = (a - jnp.where(row0, 0.0, bb)).astype(zre_ref.dtype)
    zim_ref[...] = jnp.where(row0, bb, xr * kim + xi * kre).astype(zim_ref.dtype)


def _dft_fwd(fwd, u, tm, kf=None, order=0):
    n_fft, n_t = fwd.shape
    half = n_fft // 2
    cols = u.shape[1]
    nblk = half // tm
    tn = HY_WIDTH
    in_specs = [pl.BlockSpec((tm, n_t), lambda i, j: (i, 0)),
                pl.BlockSpec((tm, n_t), lambda i, j: (i + nblk, 0)),
                pl.BlockSpec((n_t, tn), lambda i, j: (0, j))]
    args = [fwd, fwd, u]
    if kf is None:
        body, odt = _dft_fwd_body, F32
    else:
        body, odt = _dft_fwd_mul_body, BF16
        in_specs += [pl.BlockSpec((tm, tn), lambda i, j: (i, order)), pl.BlockSpec((tm, tn), lambda i, j: (i, order))]
        args += [kf[0], kf[1]]
    out = pl.BlockSpec((tm, tn), lambda i, j: (i, j))
    return pl.pallas_call(
        body,
        grid=(nblk, cols // tn),
        in_specs=in_specs,
        out_specs=[out, out],
        out_shape=[jax.ShapeDtypeStruct((half, cols), odt)] * 2,
        compiler_params=_cparams("parallel", "parallel"),
    )(*args)


def _dft_inv_body(gre_ref, gim_ref, zre_ref, zim_ref, u_ref, x_ref, bias_ref, *out_refs, inv_n):
    y = (jnp.dot(gre_ref[...], zre_ref[...], preferred_element_type=F32)
         + jnp.dot(gim_ref[...], zim_ref[...], preferred_element_type=F32)) * inv_n
    o = x_ref[...] * (y + u_ref[...] * bias_ref[...])
    if len(out_refs) == 2:
        out_refs[0][...] = o
        out_refs[1][...] = o.astype(BF16)
    else:
        out_refs[0][0] = o.astype(BF16)


def _dft_inv(g_re, g_im, zre, zim, u, xg, bias, tm, batch, last):
    n_t, half = g_re.shape
    cols = u.shape[1]
    tn = HY_WIDTH
    gsp = pl.BlockSpec((tm, half), lambda i, j: (i, 0))
    zsp = pl.BlockSpec((half, tn), lambda i, j: (0, j))
    tsp = pl.BlockSpec((tm, tn), lambda i, j: (i, j))
    if last:
        out_specs = [pl.BlockSpec((1, tm, tn), lambda i, j: (j, i, 0))]
        out_shape = [jax.ShapeDtypeStruct((batch, n_t, tn), BF16)]
    else:
        out_specs = [tsp, tsp]
        out_shape = [jax.ShapeDtypeStruct((n_t, cols), F32), jax.ShapeDtypeStruct((n_t, cols), BF16)]
    return pl.pallas_call(
        functools.partial(_dft_inv_body, inv_n=1.0 / (2 * half)),
        grid=(n_t // tm, cols // tn),
        in_specs=[gsp, gsp, zsp, zsp, tsp, tsp, pl.BlockSpec((1, tn), lambda i, j: (0, 0))],
        out_specs=out_specs,
        out_shape=out_shape,
        compiler_params=_cparams("parallel", "parallel"),
    )(g_re, g_im, zre, zim, u, xg, bias)


def _hyena(p_hy, conv_w, conv_b, hy_bias, kf, mats, tm_prep, tm_f, tm_t):
    b = p_hy.shape[0]
    fwd, g_re, g_im = mats
    v, vb, x1, x2 = _hyprep(p_hy, conv_w, conv_b, tm_prep)
    zre, zim = _dft_fwd(fwd, vb, tm_f, kf, 0)
    y1, y1b = _dft_inv(g_re, g_im, zre, zim, v, x1, hy_bias[0:1], tm_t, b, False)
    zre, zim = _dft_fwd(fwd, y1b, tm_f, kf, 1)
    (hy,) = _dft_inv(g_re, g_im, zre, zim, y1, x2, hy_bias[1:2], tm_t, b, True)
    return hy


def _attn_body(q_ref, k_ref, vt_ref, lam_ref, sub_ref, go_ref, do_ref, *, lam_init, group):
    tq = q_ref.shape[1]
    nkb, _, kb = vt_ref.shape[1:]
    lp = lam_ref[...]
    lam = (jnp.exp(jnp.sum(lp[0:1] * lp[1:2], axis=1, keepdims=True))
           - jnp.exp(jnp.sum(lp[2:3] * lp[3:4], axis=1, keepdims=True)) + lam_init)

    def attend(maps):
        def score(i, qm, kcol, kw):
            off = pl.multiple_of(i * kb, kb)
            return lax.dot_general(k_ref[0, pl.ds(off, kb), kcol:kcol + kw], qm, (((1,), (1,)), ((), ())),
                                   preferred_element_type=F32)

        def update(i, s, m, acc, vrow):
            vblk = vt_ref[0, i, vrow:vrow + V_HEAD_ROWS, :]
            m_new = jnp.maximum(m, jnp.max(s, axis=0, keepdims=True))
            alpha = jnp.exp2(m - m_new)
            p = jnp.exp2((s - m_new).astype(BF16))
            return m_new, alpha * acc + jnp.dot(vblk, p, preferred_element_type=F32)

        def body(i, state):
            scores = [score(i, qm, kcol, kw) for (qm, kcol, kw, _) in maps]
            return tuple(update(i, s, m, acc, vrow) for (_, _, _, vrow), s, (m, acc) in zip(maps, scores, state))

        init = tuple((jnp.full((1, tq), NEG_BIG, F32), jnp.zeros((V_HEAD_ROWS, tq), F32)) for _ in maps)
        return [acc[:HEAD_DIM] / acc[HEAD_DIM:HEAD_DIM + 1] for (_, acc) in lax.fori_loop(0, nkb, body, init)]

    lane = lax.broadcasted_iota(jnp.int32, (tq, LANES), 1)
    lane_d = lax.broadcasted_iota(jnp.int32, (tq, DIFF_Q_COLS), 1)
    dq = q_ref[0, :, GQA_Q_COLS:]
    maps = []
    for c in range(GQA_Q_COLS // LANES):
        qc = q_ref[0, :, c * LANES:(c + 1) * LANES]
        for j in range(GQA_KV_HEADS):
            maps.append((jnp.where((lane // HEAD_DIM) == j, qc, jnp.zeros_like(qc)), 0, GQA_KV_COLS, j * V_HEAD_ROWS))
    for h in range(DIFF_HEADS):
        for cmap in range(2):
            qm = jnp.where((lane_d // DIFF_QK_DIM) == 2 * h + cmap, dq, jnp.zeros_like(dq))
            maps.append((qm, GQA_KV_COLS, DIFF_Q_COLS, (GQA_KV_HEADS + h) * V_HEAD_ROWS))
    outs = []
    for g in range(0, len(maps), group):
        outs += attend(maps[g:g + group])

    n_gqa = GQA_HEADS
    for c in range(GQA_Q_COLS // LANES):
        pair = jnp.concatenate(outs[GQA_KV_HEADS * c:GQA_KV_HEADS * (c + 1)], axis=0)
        go_ref[0, :, c * LANES:(c + 1) * LANES] = pair.T.astype(BF16)
    sub = sub_ref[...] * (1.0 - lam_init)
    heads = []
    for h in range(DIFF_HEADS):
        d = outs[n_gqa + 2 * h] - lam * outs[n_gqa + 2 * h + 1]
        ms = jnp.mean(d * d, axis=0, keepdims=True)
        heads.append(d * lax.rsqrt(ms + NORM_EPS) * sub)
    do_ref[0] = jnp.concatenate(heads, axis=0).T.astype(BF16)


def _attention(q, k, vt, lam_p, sub, lam_init, tq, group=8):
    b, t, _ = q.shape
    lk = k.shape[1]
    row = lambda bi, i: (bi, i, 0)
    sub_t = jnp.broadcast_to(sub.reshape(DIFF_V_DIM, 1), (DIFF_V_DIM, tq))
    return pl.pallas_call(
        functools.partial(_attn_body, lam_init=lam_init, group=group),
        grid=(b, t // tq),
        in_specs=[pl.BlockSpec((1, tq, Q_COLS), row),
                  pl.BlockSpec((1, lk, KV_COLS), lambda bi, i: (bi, 0, 0)),
                  pl.BlockSpec((1,) + vt.shape[1:], lambda bi, i: (bi, 0, 0, 0)),
                  pl.BlockSpec(lam_p.shape, lambda bi, i: (0, 0)),
                  pl.BlockSpec((DIFF_V_DIM, tq), lambda bi, i: (0, 0))],
        out_specs=[pl.BlockSpec((1, tq, GQA_Q_COLS), row), pl.BlockSpec((1, tq, DIFF_V_COLS), row)],
        out_shape=[jax.ShapeDtypeStruct((b, t, GQA_Q_COLS), BF16), jax.ShapeDtypeStruct((b, t, DIFF_V_COLS), BF16)],
        compiler_params=_cparams("parallel", "parallel"),
    )(q, k, vt, lam_p, sub_t)


def _outproj_body(hy_ref, go_ref, do_ref, w_ref, x_ref, g1_ref, gpost_ref, gpre_ref, sc_ref, sh_ref,
                  wr_ref, br_ref, xm_ref, h2_ref, topi_ref, topw_ref, cnt_ref):
    y = (jnp.dot(hy_ref[0], w_ref[:HY_WIDTH], preferred_element_type=F32)
         + jnp.dot(go_ref[0], w_ref[HY_WIDTH:HY_WIDTH + GQA_Q_COLS], preferred_element_type=F32)
         + jnp.dot(do_ref[0], w_ref[HY_WIDTH + GQA_Q_COLS:], preferred_element_type=F32))
    xm = x_ref[0] + g1_ref[0] * (_rms(y) * gpost_ref[...])
    xm_ref[0] = xm
    h2 = _rms(xm) * gpre_ref[...] * (1.0 + sc_ref[0]) + sh_ref[0]
    h2_ref[0] = h2

    logits = jnp.dot(h2, wr_ref[...], precision=HIGHEST, preferred_element_type=F32) + br_ref[...]
    lane = lax.broadcasted_iota(jnp.int32, logits.shape, 1)
    topi = jnp.zeros(logits.shape, jnp.int32)
    topw = jnp.zeros_like(logits)
    chosen = jnp.zeros_like(logits)
    denom = jnp.zeros((logits.shape[0], 1), F32)
    top = None
    for k in range(TOP_K):
        mx = jnp.max(logits, axis=1, keepdims=True)
        first = jnp.min(jnp.where(logits == mx, lane, LANES), axis=1, keepdims=True)
        sel = lane == first
        top = mx if top is None else top
        e = jnp.exp(mx - top)
        topi = jnp.where(lane == k, first, topi)
        topw = jnp.where(lane == k, e, topw)
        chosen = jnp.where(sel, 1.0, chosen)
        denom = denom + e
        logits = jnp.where(sel, NEG_BIG, logits)
    topi_ref[0] = topi[:, :TOP_K]
    topw_ref[0] = (topw / denom)[:, :TOP_K]

    @pl.when((pl.program_id(0) == 0) & (pl.program_id(1) == 0))
    def _():
        cnt_ref[...] = jnp.zeros_like(cnt_ref)
    cnt_ref[...] += jnp.sum(chosen, axis=0, keepdims=True)


def _outproj(hy, go, do, w_out, x, g1, gpost, gpre, sc2, sh2, wr, br, tm):
    b, t, d = x.shape
    row = lambda bi, i: (bi, i, 0)
    per_b = lambda bi, i: (bi, 0, 0)
    const = lambda bi, i: (0, 0)
    vec = pl.BlockSpec((1, 1, d), per_b)
    return pl.pallas_call(
        _outproj_body,
        grid=(b, t // tm),
        in_specs=[pl.BlockSpec((1, tm, HY_WIDTH), row), pl.BlockSpec((1, tm, GQA_Q_COLS), row),
                  pl.BlockSpec((1, tm, DIFF_V_COLS), row), pl.BlockSpec(w_out.shape, const),
                  pl.BlockSpec((1, tm, d), row), vec, pl.BlockSpec((1, d), const), pl.BlockSpec((1, d), const),
                  vec, vec, pl.BlockSpec((d, LANES), const), pl.BlockSpec((1, LANES), const)],
        out_specs=[pl.BlockSpec((1, tm, d), row), pl.BlockSpec((1, tm, d), row),
                   pl.BlockSpec((1, tm, TOP_K), row), pl.BlockSpec((1, tm, TOP_K), row),
                   pl.BlockSpec((SUBLANES, LANES), const)],
        out_shape=[jax.ShapeDtypeStruct((b, t, d), F32), jax.ShapeDtypeStruct((b, t, d), F32),
                   jax.ShapeDtypeStruct((b, t, TOP_K), jnp.int32), jax.ShapeDtypeStruct((b, t, TOP_K), F32),
                   jax.ShapeDtypeStruct((SUBLANES, LANES), F32)],
        compiler_params=_cparams("arbitrary", "arbitrary"),
    )(hy, go, do, w_out, x, g1, gpost, gpre, sc2, sh2, wr, br)


EXPERT_TILE = 256


def _positions_body(topi_ref, base_ref, pos_ref, carry_ref):
    tm = topi_ref.shape[0]

    @pl.when(pl.program_id(0) == 0)
    def _():
        carry_ref[...] = jnp.zeros_like(carry_ref)

    topi = topi_ref[...]
    lane = lax.broadcasted_iota(jnp.int32, (tm, LANES), 1)
    hits = [lane == topi[:, k:k + 1] for k in range(TOP_K)]
    chosen = jnp.zeros((tm, LANES), F32)
    for hit in hits:
        chosen = jnp.where(hit, 1.0, chosen)
    r = lax.broadcasted_iota(jnp.int32, (tm, tm), 0)
    c = lax.broadcasted_iota(jnp.int32, (tm, tm), 1)
    before = jnp.where(c < r, 1.0, 0.0).astype(BF16)
    rank = jnp.dot(before, chosen.astype(BF16), preferred_element_type=F32) + carry_ref[0:1, :] + base_ref[...]
    pos = jnp.zeros((tm, LANES), F32)
    for k, hit in enumerate(hits):
        pos = jnp.where(lane == k, jnp.sum(jnp.where(hit, rank, 0.0), axis=1, keepdims=True), pos)
    pos_ref[...] = pos[:, :TOP_K].astype(jnp.int32)
    carry_ref[...] += jnp.sum(chosen, axis=0, keepdims=True)


def _positions(topi, base, tm):
    n = topi.shape[0]
    return pl.pallas_call(
        _positions_body,
        grid=(n // tm,),
        in_specs=[pl.BlockSpec((tm, TOP_K), lambda i: (i, 0)), pl.BlockSpec((1, LANES), lambda i: (0, 0))],
        out_specs=pl.BlockSpec((tm, TOP_K), lambda i: (i, 0)),
        out_shape=jax.ShapeDtypeStruct((n, TOP_K), jnp.int32),
        scratch_shapes=[pltpu.VMEM((SUBLANES, LANES), F32)],
        compiler_params=_cparams("arbitrary"),
    )(topi, base)


def _row_copy(src, src_row, dst, dst_row, sem):
    return pltpu.make_async_copy(src.at[pl.ds(src_row, 1)], dst.at[pl.ds(dst_row, 1)], sem)


def _scatter_body(seg_ref, pos_ref, h_ref, xs_ref, zero_ref, sem, zsem):
    i = pl.program_id(0)
    ts = pos_ref.shape[1]
    n_exp = seg_ref.shape[1]

    @pl.when(i == 0)
    def _():
        zero_ref[...] = jnp.zeros_like(zero_ref)
        for wait in (False, True):
            def per_expert(e, _):
                def per_row(r, _):
                    cp = _row_copy(zero_ref, 0, xs_ref, seg_ref[0, e] + r, zsem)
                    cp.wait() if wait else cp.start()
                    return 0
                return lax.fori_loop(seg_ref[1, e], seg_ref[2, e], per_row, 0)
            lax.fori_loop(0, n_exp, per_expert, 0)

    for wait in (False, True):
        def per_token(t, _):
            for k in range(TOP_K):
                cp = _row_copy(h_ref, i * ts + t, xs_ref, pos_ref[k, t], sem)
                cp.wait() if wait else cp.start()
            return 0
        lax.fori_loop(0, ts, per_token, 0)


def _scatter(seg, pos_t, h2, n_rows, ts):
    n, d = h2.shape
    return pl.pallas_call(
        _scatter_body,
        grid_spec=pltpu.PrefetchScalarGridSpec(
            num_scalar_prefetch=1,
            grid=(n // ts,),
            in_specs=[pl.BlockSpec((TOP_K, ts), lambda i, seg: (0, i), memory_space=pltpu.SMEM),
                      pl.BlockSpec(memory_space=pl.ANY)],
            out_specs=pl.BlockSpec(memory_space=pl.ANY),
            scratch_shapes=[pltpu.VMEM((SUBLANES, d), F32), pltpu.SemaphoreType.DMA(()), pltpu.SemaphoreType.DMA(())],
        ),
        out_shape=jax.ShapeDtypeStruct((n_rows, d), F32),
        compiler_params=_cparams("arbitrary"),
    )(seg, pos_t, h2)


def _expert_body(te_ref, na_ref, x_ref, wgu_ref, bgu_ref, wd_ref, bd_ref, y_ref):
    ff = wd_ref.shape[1]

    @pl.when(pl.program_id(0) < na_ref[0])
    def _():
        gu = jnp.dot(x_ref[...].astype(BF16), wgu_ref[0], preferred_element_type=F32) + bgu_ref[0]
        gate = jnp.minimum(gu[:, :ff], SWIGLU_LIMIT)
        up = jnp.clip(gu[:, ff:], -SWIGLU_LIMIT, SWIGLU_LIMIT)
        act = (up + 1.0) * gate / (1.0 + jnp.exp(-SWIGLU_ALPHA * gate))
        y_ref[...] = jnp.dot(act.astype(BF16), wd_ref[0], preferred_element_type=F32) + bd_ref[0]


def _experts(tile_expert, n_active, xs, wgu, bgu, wd, bd):
    n_rows, d = xs.shape
    n_exp, _, ff2 = wgu.shape
    ff = ff2 // 2
    tile = lambda i, te, na: (jnp.minimum(i, na[0] - 1), 0)
    wsel = lambda i, te, na: (te[jnp.minimum(i, na[0] - 1)], 0, 0)
    return pl.pallas_call(
        _expert_body,
        grid_spec=pltpu.PrefetchScalarGridSpec(
            num_scalar_prefetch=2,
            grid=(n_rows // EXPERT_TILE,),
            in_specs=[pl.BlockSpec((EXPERT_TILE, d), tile),
                      pl.BlockSpec((1, d, ff2), wsel), pl.BlockSpec((1, 1, ff2), wsel),
                      pl.BlockSpec((1, ff, d), wsel), pl.BlockSpec((1, 1, d), wsel)],
            out_specs=pl.BlockSpec((EXPERT_TILE, d), tile),
        ),
        out_shape=jax.ShapeDtypeStruct((n_rows, d), F32),
        compiler_params=_cparams("arbitrary"),
    )(tile_expert, n_active, xs, wgu, bgu.reshape(n_exp, 1, ff2), wd, bd.reshape(n_exp, 1, d))


def _combine_body(pos_ref, w_ref, x_ref, g2_ref, gpost_ref, ys_ref, o_ref, buf, sem):
    tc = w_ref.shape[0]
    for wait in (False, True):
        def per_token(t, _):
            for k in range(TOP_K):
                cp = pltpu.make_async_copy(ys_ref.at[pl.ds(pos_ref[k, t], 1)], buf.at[k, pl.ds(t, 1)], sem)
                cp.wait() if wait else cp.start()
            return 0
        lax.fori_loop(0, tc, per_token, 0)
    w = w_ref[...]
    y = w[:, 0:1] * buf[0]
    for k in range(1, TOP_K):
        y = y + w[:, k:k + 1] * buf[k]
    o_ref[...] = x_ref[...] + g2_ref[0] * (_rms(y) * gpost_ref[...])


def _combine(pos_t, topw, xm, g2, gpost, ys, tokens_per_batch, tc):
    n, d = xm.shape
    per_tile = tokens_per_batch // tc
    tok = lambda i: (i, 0)
    return pl.pallas_call(
        _combine_body,
        grid=(n // tc,),
        in_specs=[pl.BlockSpec((TOP_K, tc), lambda i: (0, i), memory_space=pltpu.SMEM),
                  pl.BlockSpec((tc, TOP_K), tok), pl.BlockSpec((tc, d), tok),
                  pl.BlockSpec((1, 1, d), lambda i: (i // per_tile, 0, 0)), pl.BlockSpec((1, d), lambda i: (0, 0)),
                  pl.BlockSpec(memory_space=pl.ANY)],
        out_specs=pl.BlockSpec((tc, d), tok),
        out_shape=jax.ShapeDtypeStruct((n, d), F32),
        scratch_shapes=[pltpu.VMEM((TOP_K, tc, d), F32), pltpu.SemaphoreType.DMA(())],
        compiler_params=_cparams("arbitrary"),
    )(pos_t, topw, xm, g2, gpost, ys)


def _moe(h2, topi, topw, counts, wgu, bgu, wd, bd, xm, g2, gpost, tm):
    b, t, d = xm.shape
    n_exp = wgu.shape[0]
    n = b * t
    cnt = counts[0, :n_exp].astype(jnp.int32)
    padded = (cnt + EXPERT_TILE - 1) // EXPERT_TILE * EXPERT_TILE
    ends = jnp.cumsum(padded)
    first = ends - padded
    n_rows = n * TOP_K + n_exp * EXPERT_TILE
    tile_start = jnp.arange(n_rows // EXPERT_TILE, dtype=jnp.int32) * EXPERT_TILE
    tile_expert = jnp.minimum(jnp.sum(tile_start[:, None] >= ends[None, :], axis=1), n_exp - 1).astype(jnp.int32)
    n_active = (ends[-1:] // EXPERT_TILE).astype(jnp.int32)
    base = jnp.zeros((1, LANES), F32).at[0, :n_exp].set(first.astype(F32))
    seg = jnp.stack([first, cnt, padded]).astype(jnp.int32)

    pos = _positions(topi.reshape(n, TOP_K), base, tm)
    pos_t = pos.T
    xs = _scatter(seg, pos_t, h2.reshape(n, d), n_rows, tm)
    ys = _experts(tile_expert, n_active, xs, wgu, bgu, wd, bd)
    out = _combine(pos_t, topw.reshape(n, TOP_K), xm.reshape(n, d), g2, gpost, ys, t, min(tm, 256))
    return out.reshape(b, t, d)


def _rope_tables(n_tok, dim, use_pos):
    if not use_pos:
        return jnp.ones((n_tok, LANES), F32), jnp.zeros((n_tok, LANES), F32)
    half = dim // 4
    t = jnp.arange(n_tok, dtype=jnp.int32)
    row = (t // GRID_W).astype(F32)
    col = (t % GRID_W).astype(F32)
    inv_freq = ROPE_THETA ** (-jnp.arange(half, dtype=F32) / half)
    a_row = row[:, None] * inv_freq[None, :]
    a_col = col[:, None] * inv_freq[None, :]
    cos = jnp.concatenate([jnp.cos(a_row)] * 2 + [jnp.cos(a_col)] * 2, axis=1)
    sin = jnp.concatenate([-jnp.sin(a_row), jnp.sin(a_row), -jnp.sin(a_col), jnp.sin(a_col)], axis=1)
    reps = LANES // dim
    return jnp.tile(cos, (1, reps)), jnp.tile(sin, (1, reps))


def _hyena_feats(n_t):
    t = jnp.arange(n_t, dtype=F32)
    bands = jnp.linspace(1e-4, HY_BANDS - 1, HY_BANDS, dtype=F32)
    w = (2.0 * math.pi / n_t) * t
    feats = jnp.concatenate([(t / n_t)[:, None], jnp.cos(w[:, None] * bands), -jnp.sin(w[:, None] * bands)], axis=-1)
    return jnp.pad(feats, ((0, 0), (0, LANES - feats.shape[1])))


def _tile_rows(t, cap):
    return cap if t % cap == 0 else t


def kernel(x, c, ctx, c_ctx, w_ada, b_ada, g_pre_mix, g_post_mix, g_pre_ffn, g_post_ffn, w_in, w_out, hy_conv_w,
           hy_conv_b, hy_w1, hy_b1, hy_w2, hy_b2, hy_w3, hy_freq, hy_decay, hy_bias, q_norm, k_norm, lam_q1, lam_k1,
           lam_q2, lam_k2, subln, w_router, b_router, w_gu, b_gu, w_down, b_down):
    b, s, d = x.shape
    n_ctx = ctx.shape[1]
    depth = w_ada.shape[0]
    n_exp = w_router.shape[-1]

    rows = -(-(b + 1) // SUBLANES) * SUBLANES
    cc = jnp.zeros((rows, d), F32).at[:b].set(c).at[b].set(c_ctx)
    mod = _ada(cc, w_ada, b_ada)

    head_cols = jnp.concatenate([jnp.arange(HEAD_DIM) + HEAD_DIM * h for h in _GQA_HEAD_ORDER])
    in_perm = jnp.concatenate([jnp.arange(Q_START), Q_START + head_cols, jnp.arange(Q_START + GQA_Q_COLS, IN_COLS)])
    out_perm = jnp.concatenate([jnp.arange(HY_WIDTH), HY_WIDTH + head_cols, jnp.arange(HY_WIDTH + GQA_Q_COLS, d)])
    w_in_b = w_in[:, :, in_perm].astype(BF16)
    w_out_b = w_out[:, out_perm, :].astype(BF16)
    w_gu_b = w_gu.astype(BF16)
    w_down_b = w_down.astype(BF16)
    wr_p = jnp.pad(w_router, ((0, 0), (0, 0), (0, LANES - n_exp)))
    br_p = jnp.pad(b_router, ((0, 0), (0, LANES - n_exp)), constant_values=NEG_BIG)[:, None, :]

    lane = jnp.arange(LANES)
    block_diag = (lane[:, None] // HEAD_DIM == lane[None, :] // HEAD_DIM).astype(F32)
    reps = LANES // HEAD_DIM

    tabs_lat = _rope_tables(s, HEAD_DIM, True) + _rope_tables(s, DIFF_QK_DIM, True)
    tabs_ctx = _rope_tables(n_ctx, HEAD_DIM, False) + _rope_tables(n_ctx, DIFF_QK_DIM, False)
    mats_lat = _dft_mats(s)
    feats_lat = _hyena_feats(s)
    if depth > 1:
        mats_ctx = _dft_mats(n_ctx)
        feats_ctx = _hyena_feats(n_ctx)

    tm_lat = _tile_rows(s, 512)
    tm_ctx = _tile_rows(n_ctx, 256)
    lk = n_ctx + s
    kb = 256 if (tm_lat % 256 == 0 and tm_ctx % 256 == 0) else 128

    def filters(i, feats, mats):
        w1p = jnp.pad(hy_w1[i], ((0, LANES - hy_w1.shape[1]), (0, 0)))
        k = _hyena_filters(feats, w1p, hy_b1[i][None], hy_w2[i], hy_b2[i][None], hy_w3[i], hy_freq[i][None],
                           hy_decay[i][None])
        half = mats[0].shape[0] // 2
        return _dft_fwd(mats[0], k, _tile_rows(half, 512))

    xc = ctx
    for i in range(depth):
        ctx_out = i < depth - 1
        lam_init = 0.8 - 0.6 * math.exp(-0.3 * i)
        m_lat = mod[i, :b].reshape(b, 1, 6 * d)
        sh1, sc1, g1, sh2, sc2, g2 = [m_lat[:, :, j * d:(j + 1) * d] for j in range(6)]
        m_ctx = jnp.broadcast_to(mod[i, b].reshape(1, 1, 6 * d), (b, 1, 6 * d))
        csh1, csc1, cg1, csh2, csc2, cg2 = [m_ctx[:, :, j * d:(j + 1) * d] for j in range(6)]

        qg = jnp.tile(q_norm[i], reps)[None] * (HEAD_DIM ** -0.5 * LOG2E)
        kg = jnp.tile(k_norm[i], reps)[None]
        lam_p = jnp.stack([lam_q1[i], lam_k1[i], lam_q2[i], lam_k2[i]])
        proj = functools.partial(_inproj, g=g_pre_mix[i][None], w=w_in_b[i], qg=qg, kg=kg, bd=block_diag, kb=kb)

        hy_c, q_c, k_c, v_c = proj(xc, csc1, csh1, tables=tabs_ctx, tm=tm_ctx)
        hy_l, q_l, k_l, v_l = proj(x, sc1, sh1, tables=tabs_lat, tm=tm_lat)
        k_all = jnp.concatenate([k_c, k_l], axis=1)
        v_all = jnp.concatenate([v_c, v_l], axis=1)

        attn = functools.partial(_attention, lam_p=lam_p, sub=subln[i], lam_init=lam_init)
        go_l, do_l = attn(q_l, k_all, v_all, tq=_tile_rows(s, 256))
        kf_lat = filters(i, feats_lat, mats_lat)
        half_lat = mats_lat[0].shape[0] // 2
        hyo_l = _hyena(hy_l, hy_conv_w[i], hy_conv_b[i][None], hy_bias[i], kf_lat, mats_lat,
                       tm_lat, _tile_rows(half_lat, 512), tm_lat)

        post = functools.partial(_outproj, w_out=w_out_b[i], gpost=g_post_mix[i][None], gpre=g_pre_ffn[i][None],
                                 wr=wr_p[i], br=br_p[i])
        moe = functools.partial(_moe, wgu=w_gu_b[i], bgu=b_gu[i], wd=w_down_b[i], bd=b_down[i],
                                gpost=g_post_ffn[i][None])

        xm, h2, topi, topw, cnt = post(hyo_l, go_l, do_l, x=x, g1=g1, sc2=sc2, sh2=sh2, tm=tm_lat)
        x = moe(h2, topi, topw, cnt, xm=xm, g2=g2, tm=tm_lat)

        if ctx_out:
            go_c, do_c = attn(q_c, k_c, v_c, tq=tm_ctx)
            kf_ctx = filters(i, feats_ctx, mats_ctx)
            half_ctx = mats_ctx[0].shape[0] // 2
            hyo_c = _hyena(hy_c, hy_conv_w[i], hy_conv_b[i][None], hy_bias[i], kf_ctx, mats_ctx,
                           tm_ctx, _tile_rows(half_ctx, 512), tm_ctx)
            xcm, h2c, topic, topwc, cntc = post(hyo_c, go_c, do_c, x=xc, g1=cg1, sc2=csc2, sh2=csh2, tm=tm_ctx)
            xc = moe(h2c, topic, topwc, cntc, xm=xcm, g2=cg2, tm=tm_ctx)
    return x
```

```python
import functools
import math

import jax
import jax.numpy as jnp
from jax import lax
from jax.experimental import pallas as pl
from jax.experimental.pallas import tpu as pltpu

F32 = jnp.float32
BF16 = jnp.bfloat16
HIGHEST = lax.Precision.HIGHEST

HEAD_DIM = 64
ROPE_THETA = 10000.0
NORM_EPS = 1e-6
GRID_W = 64
HY_WIDTH = 256
HY_BANDS = 16
GQA_HEADS = 8
GQA_KV_HEADS = 2
GQA_GROUP = GQA_HEADS // GQA_KV_HEADS
DIFF_HEADS = 4
DIFF_QK_DIM = 32
DIFF_V_DIM = 64
TOP_K = 4
SWIGLU_LIMIT = 7.0
SWIGLU_ALPHA = 1.702

HY_COLS = 3 * HY_WIDTH
GQA_Q_COLS = GQA_HEADS * HEAD_DIM
DIFF_Q_COLS = DIFF_HEADS * 2 * DIFF_QK_DIM
GQA_KV_COLS = GQA_KV_HEADS * HEAD_DIM
DIFF_V_COLS = DIFF_HEADS * DIFF_V_DIM
Q_COLS = GQA_Q_COLS + DIFF_Q_COLS
KV_COLS = GQA_KV_COLS + DIFF_Q_COLS
Q_START = HY_COLS
K_START = Q_START + Q_COLS
IN_COLS = K_START + 2 * KV_COLS
V_HEAD_ROWS = HEAD_DIM + 16
VT_ROWS = (KV_COLS // HEAD_DIM) * V_HEAD_ROWS

LANES = 128
SUBLANES = 8
VMEM_LIMIT_BYTES = 56 * 1024 * 1024
NEG_BIG = -1e30
LOG2E = math.log2(math.e)

_GQA_HEAD_ORDER = tuple(g + GQA_GROUP * j for g in range(GQA_GROUP) for j in range(GQA_KV_HEADS))


def _cparams(*sem):
    return pltpu.CompilerParams(dimension_semantics=sem, vmem_limit_bytes=VMEM_LIMIT_BYTES)


def _rms(x):
    return x * lax.rsqrt(jnp.mean(x * x, axis=-1, keepdims=True) + NORM_EPS)


def _ada_body(c_ref, w_ref, b_ref, o_ref):
    c = c_ref[...]
    s = c / (1.0 + jnp.exp(-c))
    o_ref[0] = jnp.dot(s.astype(BF16), w_ref[0].astype(BF16), preferred_element_type=F32) + b_ref[0]


def _ada(cc, w_ada, b_ada):
    n_layers, d, n = w_ada.shape
    rows = cc.shape[0]
    tn = 512
    return pl.pallas_call(
        _ada_body,
        grid=(n_layers, n // tn),
        in_specs=[pl.BlockSpec((rows, d), lambda l, j: (0, 0)),
                  pl.BlockSpec((1, d, tn), lambda l, j: (l, 0, j)),
                  pl.BlockSpec((1, 1, tn), lambda l, j: (l, 0, j))],
        out_specs=pl.BlockSpec((1, rows, tn), lambda l, j: (l, 0, j)),
        out_shape=jax.ShapeDtypeStruct((n_layers, rows, n), F32),
        compiler_params=_cparams("parallel", "parallel"),
    )(cc, w_ada, b_ada.reshape(n_layers, 1, n))


def _rope(xc, cos, sin, half, lane):
    fwd = pltpu.roll(xc, LANES - half, 1)
    bwd = pltpu.roll(xc, half, 1)
    sw = jnp.where((lane % (2 * half)) < half, fwd, bwd)
    return xc * cos + sw * sin


def _inproj_body(x_ref, sc_ref, sh_ref, g_ref, w_ref, qg_ref, kg_ref, bd_ref,
                 cg_ref, sg_ref, cd_ref, sd_ref, hy_ref, q_ref, k_ref, vt_ref):
    tm = x_ref.shape[1]
    h = _rms(x_ref[0]) * g_ref[...]
    h = h * (1.0 + sc_ref[0]) + sh_ref[0]
    p = jnp.dot(h.astype(BF16), w_ref[...], preferred_element_type=F32)
    hy_ref[0] = p[:, :HY_COLS]

    lane = lax.broadcasted_iota(jnp.int32, (tm, LANES), 1)
    cg, sg, cd, sd = cg_ref[...], sg_ref[...], cd_ref[...], sd_ref[...]
    bd = bd_ref[...]

    def head_norm(xc, gain):
        ms = jnp.dot(xc * xc, bd, precision=HIGHEST, preferred_element_type=F32) * (1.0 / HEAD_DIM)
        return xc * lax.rsqrt(ms + NORM_EPS) * gain

    for c in range(GQA_Q_COLS // LANES):
        xc = p[:, Q_START + c * LANES: Q_START + (c + 1) * LANES]
        xc = head_norm(xc, qg_ref[...])
        q_ref[0, :, c * LANES:(c + 1) * LANES] = _rope(xc, cg, sg, HEAD_DIM // 4, lane).astype(BF16)
    dscale = DIFF_QK_DIM ** -0.5 * LOG2E
    for c in range(DIFF_Q_COLS // LANES):
        o = Q_START + GQA_Q_COLS + c * LANES
        xc = p[:, o:o + LANES] * dscale
        q_ref[0, :, GQA_Q_COLS + c * LANES: GQA_Q_COLS + (c + 1) * LANES] = (
            _rope(xc, cd, sd, DIFF_QK_DIM // 4, lane).astype(BF16))

    gk = head_norm(p[:, K_START:K_START + GQA_KV_COLS], kg_ref[...])
    k_ref[0, :, :GQA_KV_COLS] = _rope(gk, cg, sg, HEAD_DIM // 4, lane).astype(BF16)
    dk0 = K_START + 2 * GQA_KV_COLS
    for c in range(DIFF_Q_COLS // LANES):
        xc = p[:, dk0 + c * LANES: dk0 + (c + 1) * LANES]
        k_ref[0, :, GQA_KV_COLS + c * LANES: GQA_KV_COLS + (c + 1) * LANES] = (
            _rope(xc, cd, sd, DIFF_QK_DIM // 4, lane).astype(BF16))
    kb = vt_ref.shape[3]
    for sblk in range(tm // kb):
        rows = slice(sblk * kb, (sblk + 1) * kb)
        vals = jnp.concatenate([p[rows, K_START + GQA_KV_COLS:K_START + 2 * GQA_KV_COLS],
                                p[rows, dk0 + DIFF_Q_COLS:]], axis=1)
        vals_t = vals.T.astype(BF16)
        ones = jnp.ones((V_HEAD_ROWS - HEAD_DIM, kb), BF16)
        for hd in range(KV_COLS // HEAD_DIM):
            vt_ref[0, sblk, hd * V_HEAD_ROWS:hd * V_HEAD_ROWS + HEAD_DIM, :] = vals_t[hd * HEAD_DIM:(hd + 1) * HEAD_DIM]
            vt_ref[0, sblk, hd * V_HEAD_ROWS + HEAD_DIM:(hd + 1) * V_HEAD_ROWS, :] = ones


def _inproj(x, scale, shift, g, w, qg, kg, bd, tables, tm, kb):
    b, t, d = x.shape
    row = lambda bi, i: (bi, i, 0)
    per_b = lambda bi, i: (bi, 0, 0)
    const = lambda bi, i: (0, 0)
    tab = pl.BlockSpec((tm, LANES), lambda bi, i: (i, 0))
    return pl.pallas_call(
        _inproj_body,
        grid=(b, t // tm),
        in_specs=[pl.BlockSpec((1, tm, d), row),
                  pl.BlockSpec((1, 1, d), per_b), pl.BlockSpec((1, 1, d), per_b),
                  pl.BlockSpec((1, d), const),
                  pl.BlockSpec((d, IN_COLS), const),
                  pl.BlockSpec((1, LANES), const), pl.BlockSpec((1, LANES), const),
                  pl.BlockSpec((LANES, LANES), const),
                  tab, tab, tab, tab],
        out_specs=[pl.BlockSpec((1, tm, HY_COLS), row), pl.BlockSpec((1, tm, Q_COLS), row),
                   pl.BlockSpec((1, tm, KV_COLS), row),
                   pl.BlockSpec((1, tm // kb, VT_ROWS, kb), lambda bi, i: (bi, i, 0, 0))],
        out_shape=[jax.ShapeDtypeStruct((b, t, HY_COLS), F32), jax.ShapeDtypeStruct((b, t, Q_COLS), BF16),
                   jax.ShapeDtypeStruct((b, t, KV_COLS), BF16),
                   jax.ShapeDtypeStruct((b, t // kb, VT_ROWS, kb), BF16)],
        compiler_params=_cparams("parallel", "parallel"),
    )(x, scale, shift, g, w, qg, kg, bd, *tables)


def _filter_body(f_ref, w1_ref, b1_ref, w2_ref, b2_ref, w3_ref, fr_ref, dc_ref, o_ref):
    n = f_ref.shape[0]
    fr = fr_ref[...]
    h = jnp.sin(fr * (jnp.dot(f_ref[...], w1_ref[...], precision=HIGHEST, preferred_element_type=F32) + b1_ref[...]))
    h = jnp.sin(fr * (jnp.dot(h, w2_ref[...], precision=HIGHEST, preferred_element_type=F32) + b2_ref[...]))
    k = jnp.dot(h, w3_ref[...], precision=HIGHEST, preferred_element_type=F32)
    t = lax.broadcasted_iota(jnp.int32, k.shape, 0)
    off = jnp.abs(t - n // 2).astype(F32) * (2.0 / n)
    k = k * jnp.exp(-off * jnp.abs(dc_ref[...]))
    k = k / jnp.sum(jnp.abs(k), axis=0, keepdims=True)
    o_ref[...] = k.astype(BF16)


def _hyena_filters(feats, w1p, b1, w2, b2, w3, freq, decay):
    n = feats.shape[0]
    hid = w2.shape[0]
    cols = w3.shape[1]
    const = lambda j: (0, 0)
    return pl.pallas_call(
        _filter_body,
        grid=(cols // LANES,),
        in_specs=[pl.BlockSpec((n, LANES), const), pl.BlockSpec((LANES, hid), const),
                  pl.BlockSpec((1, hid), const), pl.BlockSpec((hid, hid), const), pl.BlockSpec((1, hid), const),
                  pl.BlockSpec((hid, LANES), lambda j: (0, j)), pl.BlockSpec((1, hid), const),
                  pl.BlockSpec((1, LANES), lambda j: (0, j))],
        out_specs=pl.BlockSpec((n, LANES), lambda j: (0, j)),
        out_shape=jax.ShapeDtypeStruct((n, cols), BF16),
        compiler_params=_cparams("parallel"),
    )(feats, w1p, b1, w2, b2, w3, freq, decay)


def _hyprep_body(p_ref, prev_ref, next_ref, w_ref, b_ref, v_ref, vb_ref, x1_ref, x2_ref, buf):
    i = pl.program_id(1)
    n = pl.num_programs(1)
    tm = p_ref.shape[1]
    buf[SUBLANES:tm + SUBLANES, :] = p_ref[0]
    buf[SUBLANES - 1:SUBLANES, :] = jnp.where(i > 0, prev_ref[0][SUBLANES - 1:SUBLANES, :], 0.0)
    buf[tm + SUBLANES:tm + SUBLANES + 1, :] = jnp.where(i < n - 1, next_ref[0][0:1, :], 0.0)
    w = w_ref[...]
    y = (buf[SUBLANES - 1:tm + SUBLANES - 1, :] * w[0:1] + buf[SUBLANES:tm + SUBLANES, :] * w[1:2]
         + buf[SUBLANES + 1:tm + SUBLANES + 1, :] * w[2:3] + b_ref[...])
    v_ref[...] = y[:, :HY_WIDTH]
    vb_ref[...] = y[:, :HY_WIDTH].astype(BF16)
    x1_ref[...] = y[:, HY_WIDTH:2 * HY_WIDTH]
    x2_ref[...] = y[:, 2 * HY_WIDTH:]


def _hyprep(p_hy, conv_w, conv_b, tm):
    b, t, _ = p_hy.shape
    nb8 = t // SUBLANES
    r8 = tm // SUBLANES
    tmaj = pl.BlockSpec((tm, HY_WIDTH), lambda bi, i: (i, bi))
    shp = lambda dt: jax.ShapeDtypeStruct((t, b * HY_WIDTH), dt)
    return pl.pallas_call(
        _hyprep_body,
        grid=(b, t // tm),
        in_specs=[pl.BlockSpec((1, tm, HY_COLS), lambda bi, i: (bi, i, 0)),
                  pl.BlockSpec((1, SUBLANES, HY_COLS), lambda bi, i: (bi, jnp.maximum(i * r8 - 1, 0), 0)),
                  pl.BlockSpec((1, SUBLANES, HY_COLS), lambda bi, i: (bi, jnp.minimum((i + 1) * r8, nb8 - 1), 0)),
                  pl.BlockSpec((3, HY_COLS), lambda bi, i: (0, 0)),
                  pl.BlockSpec((1, HY_COLS), lambda bi, i: (0, 0))],
        out_specs=[tmaj, tmaj, tmaj, tmaj],
        out_shape=[shp(F32), shp(BF16), shp(F32), shp(F32)],
        scratch_shapes=[pltpu.VMEM((tm + 2 * SUBLANES, HY_COLS), F32)],
        compiler_params=_cparams("parallel", "parallel"),
    )(p_hy, p_hy, p_hy, conv_w, conv_b)


def _dft_mats(n_t):
    n_fft = 3 * n_t // 2
    half = n_fft // 2
    t = jnp.arange(n_t, dtype=jnp.int32)
    f = jnp.arange(half, dtype=jnp.int32)
    step = 2.0 * math.pi / n_fft
    ang = ((f[:, None] * t[None, :]) % n_fft).astype(F32) * step
    alt_t = (1 - 2 * (t % 2)).astype(F32)
    f_re = jnp.cos(ang)
    f_im = jnp.where(f[:, None] == 0, alt_t[None, :], -jnp.sin(ang))
    fwd = jnp.concatenate([f_re, f_im], axis=0).astype(BF16)
    n_out = t + n_t // 2
    ang2 = ((n_out[:, None] * f[None, :]) % n_fft).astype(F32) * step
    alt_n = (1 - 2 * (n_out % 2)).astype(F32)
    g_re = jnp.where(f[None, :] == 0, 1.0, 2.0 * jnp.cos(ang2)).astype(BF16)
    g_im = jnp.where(f[None, :] == 0, alt_n[:, None], -2.0 * jnp.sin(ang2)).astype(BF16)
    return fwd, g_re, g_im


def _dft_fwd_body(fre_ref, fim_ref, u_ref, zre_ref, zim_ref):
    u = u_ref[...]
    zre_ref[...] = jnp.dot(fre_ref[...], u, preferred_element_type=F32)
    zim_ref[...] = jnp.dot(fim_ref[...], u, preferred_element_type=F32)


def _dft_fwd_mul_body(fre_ref, fim_ref, u_ref, kre_ref, kim_ref, zre_ref, zim_ref):
    u = u_ref[...]
    xr = jnp.dot(fre_ref[...], u, preferred_element_type=F32)
    xi = jnp.dot(fim_ref[...], u, preferred_element_type=F32)
    kre, kim = kre_ref[...], kim_ref[...]
    row0 = (lax.broadcasted_iota(jnp.int32, xr.shape, 0) == 0) & (pl.program_id(0) == 0)
    a = xr * kre
    bb = xi * kim
    zre_ref[...] = (a - jnp.where(row0, 0.0, bb)).astype(zre_ref.dtype)
    zim_ref[...] = jnp.where(row0, bb, xr * kim + xi * kre).astype(zim_ref.dtype)


def _dft_fwd(fwd, u, tm, kf=None, order=0):
    n_fft, n_t = fwd.shape
    half = n_fft // 2
    cols = u.shape[1]
    nblk = half // tm
    tn = HY_WIDTH
    in_specs = [pl.BlockSpec((tm, n_t), lambda i, j: (i, 0)),
                pl.BlockSpec((tm, n_t), lambda i, j: (i + nblk, 0)),
                pl.BlockSpec((n_t, tn), lambda i, j: (0, j))]
    args = [fwd, fwd, u]
    if kf is None:
        body, odt = _dft_fwd_body, F32
    else:
        body, odt = _dft_fwd_mul_body, BF16
        in_specs += [pl.BlockSpec((tm, tn), lambda i, j: (i, order)), pl.BlockSpec((tm, tn), lambda i, j: (i, order))]
        args += [kf[0], kf[1]]
    out = pl.BlockSpec((tm, tn), lambda i, j: (i, j))
    return pl.pallas_call(
        body,
        grid=(nblk, cols // tn),
        in_specs=in_specs,
        out_specs=[out, out],
        out_shape=[jax.ShapeDtypeStruct((half, cols), odt)] * 2,
        compiler_params=_cparams("parallel", "parallel"),
    )(*args)


def _dft_inv_body(gre_ref, gim_ref, zre_ref, zim_ref, u_ref, x_ref, bias_ref, *out_refs, inv_n):
    y = (jnp.dot(gre_ref[...], zre_ref[...], preferred_element_type=F32)
         + jnp.dot(gim_ref[...], zim_ref[...], preferred_element_type=F32)) * inv_n
    o = x_ref[...] * (y + u_ref[...] * bias_ref[...])
    if len(out_refs) == 2:
        out_refs[0][...] = o
        out_refs[1][...] = o.astype(BF16)
    else:
        out_refs[0][0] = o.astype(BF16)


def _dft_inv(g_re, g_im, zre, zim, u, xg, bias, tm, batch, last):
    n_t, half = g_re.shape
    cols = u.shape[1]
    tn = HY_WIDTH
    gsp = pl.BlockSpec((tm, half), lambda i, j: (i, 0))
    zsp = pl.BlockSpec((half, tn), lambda i, j: (0, j))
    tsp = pl.BlockSpec((tm, tn), lambda i, j: (i, j))
    if last:
        out_specs = [pl.BlockSpec((1, tm, tn), lambda i, j: (j, i, 0))]
        out_shape = [jax.ShapeDtypeStruct((batch, n_t, tn), BF16)]
    else:
        out_specs = [tsp, tsp]
        out_shape = [jax.ShapeDtypeStruct((n_t, cols), F32), jax.ShapeDtypeStruct((n_t, cols), BF16)]
    return pl.pallas_call(
        functools.partial(_dft_inv_body, inv_n=1.0 / (2 * half)),
        grid=(n_t // tm, cols // tn),
        in_specs=[gsp, gsp, zsp, zsp, tsp, tsp, pl.BlockSpec((1, tn), lambda i, j: (0, 0))],
        out_specs=out_specs,
        out_shape=out_shape,
        compiler_params=_cparams("parallel", "parallel"),
    )(g_re, g_im, zre, zim, u, xg, bias)


def _hyena(p_hy, conv_w, conv_b, hy_bias, kf, mats, tm_prep, tm_f, tm_t):
    b = p_hy.shape[0]
    fwd, g_re, g_im = mats
    v, vb, x1, x2 = _hyprep(p_hy, conv_w, conv_b, tm_prep)
    zre, zim = _dft_fwd(fwd, vb, tm_f, kf, 0)
    y1, y1b = _dft_inv(g_re, g_im, zre, zim, v, x1, hy_bias[0:1], tm_t, b, False)
    zre, zim = _dft_fwd(fwd, y1b, tm_f, kf, 1)
    (hy,) = _dft_inv(g_re, g_im, zre, zim, y1, x2, hy_bias[1:2], tm_t, b, True)
    return hy


def _attn_body(q_ref, k_ref, vt_ref, lam_ref, sub_ref, go_ref, do_ref, *, lam_init, group):
    tq = q_ref.shape[1]
    nkb, _, kb = vt_ref.shape[1:]
    lp = lam_ref[...]
    lam = (jnp.exp(jnp.sum(lp[0:1] * lp[1:2], axis=1, keepdims=True))
           - jnp.exp(jnp.sum(lp[2:3] * lp[3:4], axis=1, keepdims=True)) + lam_init)

    def attend(maps):
        def score(i, qm, kcol, kw):
            off = pl.multiple_of(i * kb, kb)
            return lax.dot_general(k_ref[0, pl.ds(off, kb), kcol:kcol + kw], qm, (((1,), (1,)), ((), ())),
                                   preferred_element_type=F32)

        def update(i, s, m, acc, vrow):
            vblk = vt_ref[0, i, vrow:vrow + V_HEAD_ROWS, :]
            m_new = jnp.maximum(m, jnp.max(s, axis=0, keepdims=True))
            alpha = jnp.exp2(m - m_new)
            p = jnp.exp2((s - m_new).astype(BF16))
            return m_new, alpha * acc + jnp.dot(vblk, p, preferred_element_type=F32)

        def body(i, state):
            scores = [score(i, qm, kcol, kw) for (qm, kcol, kw, _) in maps]
            return tuple(update(i, s, m, acc, vrow) for (_, _, _, vrow), s, (m, acc) in zip(maps, scores, state))

        init = tuple((jnp.full((1, tq), NEG_BIG, F32), jnp.zeros((V_HEAD_ROWS, tq), F32)) for _ in maps)
        return [acc[:HEAD_DIM] / acc[HEAD_DIM:HEAD_DIM + 1] for (_, acc) in lax.fori_loop(0, nkb, body, init)]

    lane = lax.broadcasted_iota(jnp.int32, (tq, LANES), 1)
    lane_d = lax.broadcasted_iota(jnp.int32, (tq, DIFF_Q_COLS), 1)
    dq = q_ref[0, :, GQA_Q_COLS:]
    maps = []
    for c in range(GQA_Q_COLS // LANES):
        qc = q_ref[0, :, c * LANES:(c + 1) * LANES]
        for j in range(GQA_KV_HEADS):
            maps.append((jnp.where((lane // HEAD_DIM) == j, qc, jnp.zeros_like(qc)), 0, GQA_KV_COLS, j * V_HEAD_ROWS))
    for h in range(DIFF_HEADS):
        for cmap in range(2):
            qm = jnp.where((lane_d // DIFF_QK_DIM) == 2 * h + cmap, dq, jnp.zeros_like(dq))
            maps.append((qm, GQA_KV_COLS, DIFF_Q_COLS, (GQA_KV_HEADS + h) * V_HEAD_ROWS))
    outs = []
    for g in range(0, len(maps), group):
        outs += attend(maps[g:g + group])

    n_gqa = GQA_HEADS
    for c in range(GQA_Q_COLS // LANES):
        pair = jnp.concatenate(outs[GQA_KV_HEADS * c:GQA_KV_HEADS * (c + 1)], axis=0)
        go_ref[0, :, c * LANES:(c + 1) * LANES] = pair.T.astype(BF16)
    sub = sub_ref[...] * (1.0 - lam_init)
    heads = []
    for h in range(DIFF_HEADS):
        d = outs[n_gqa + 2 * h] - lam * outs[n_gqa + 2 * h + 1]
        ms = jnp.mean(d * d, axis=0, keepdims=True)
        heads.append(d * lax.rsqrt(ms + NORM_EPS) * sub)
    do_ref[0] = jnp.concatenate(heads, axis=0).T.astype(BF16)


def _attention(q, k, vt, lam_p, sub, lam_init, tq, group=8):
    b, t, _ = q.shape
    lk = k.shape[1]
    row = lambda bi, i: (bi, i, 0)
    sub_t = jnp.broadcast_to(sub.reshape(DIFF_V_DIM, 1), (DIFF_V_DIM, tq))
    return pl.pallas_call(
        functools.partial(_attn_body, lam_init=lam_init, group=group),
        grid=(b, t // tq),
        in_specs=[pl.BlockSpec((1, tq, Q_COLS), row),
                  pl.BlockSpec((1, lk, KV_COLS), lambda bi, i: (bi, 0, 0)),
                  pl.BlockSpec((1,) + vt.shape[1:], lambda bi, i: (bi, 0, 0, 0)),
                  pl.BlockSpec(lam_p.shape, lambda bi, i: (0, 0)),
                  pl.BlockSpec((DIFF_V_DIM, tq), lambda bi, i: (0, 0))],
        out_specs=[pl.BlockSpec((1, tq, GQA_Q_COLS), row), pl.BlockSpec((1, tq, DIFF_V_COLS), row)],
        out_shape=[jax.ShapeDtypeStruct((b, t, GQA_Q_COLS), BF16), jax.ShapeDtypeStruct((b, t, DIFF_V_COLS), BF16)],
        compiler_params=_cparams("parallel", "parallel"),
    )(q, k, vt, lam_p, sub_t)


def _outproj_body(hy_ref, go_ref, do_ref, w_ref, x_ref, g1_ref, gpost_ref, gpre_ref, sc_ref, sh_ref,
                  wr_ref, br_ref, xm_ref, h2_ref, topi_ref, topw_ref, cnt_ref):
    y = (jnp.dot(hy_ref[0], w_ref[:HY_WIDTH], preferred_element_type=F32)
         + jnp.dot(go_ref[0], w_ref[HY_WIDTH:HY_WIDTH + GQA_Q_COLS], preferred_element_type=F32)
         + jnp.dot(do_ref[0], w_ref[HY_WIDTH + GQA_Q_COLS:], preferred_element_type=F32))
    xm = x_ref[0] + g1_ref[0] * (_rms(y) * gpost_ref[...])
    xm_ref[0] = xm
    h2 = _rms(xm) * gpre_ref[...] * (1.0 + sc_ref[0]) + sh_ref[0]
    h2_ref[0] = h2

    logits = jnp.dot(h2, wr_ref[...], precision=HIGHEST, preferred_element_type=F32) + br_ref[...]
    lane = lax.broadcasted_iota(jnp.int32, logits.shape, 1)
    topi = jnp.zeros(logits.shape, jnp.int32)
    topw = jnp.zeros_like(logits)
    chosen = jnp.zeros_like(logits)
    denom = jnp.zeros((logits.shape[0], 1), F32)
    top = None
    for k in range(TOP_K):
        mx = jnp.max(logits, axis=1, keepdims=True)
        first = jnp.min(jnp.where(logits == mx, lane, LANES), axis=1, keepdims=True)
        sel = lane == first
        top = mx if top is None else top
        e = jnp.exp(mx - top)
        topi = jnp.where(lane == k, first, topi)
        topw = jnp.where(lane == k, e, topw)
        chosen = jnp.where(sel, 1.0, chosen)
        denom = denom + e
        logits = jnp.where(sel, NEG_BIG, logits)
    topi_ref[0] = topi[:, :TOP_K]
    topw_ref[0] = (topw / denom)[:, :TOP_K]

    @pl.when((pl.program_id(0) == 0) & (pl.program_id(1) == 0))
    def _():
        cnt_ref[...] = jnp.zeros_like(cnt_ref)
    cnt_ref[...] += jnp.sum(chosen, axis=0, keepdims=True)


def _outproj(hy, go, do, w_out, x, g1, gpost, gpre, sc2, sh2, wr, br, tm):
    b, t, d = x.shape
    row = lambda bi, i: (bi, i, 0)
    per_b = lambda bi, i: (bi, 0, 0)
    const = lambda bi, i: (0, 0)
    vec = pl.BlockSpec((1, 1, d), per_b)
    return pl.pallas_call(
        _outproj_body,
        grid=(b, t // tm),
        in_specs=[pl.BlockSpec((1, tm, HY_WIDTH), row), pl.BlockSpec((1, tm, GQA_Q_COLS), row),
                  pl.BlockSpec((1, tm, DIFF_V_COLS), row), pl.BlockSpec(w_out.shape, const),
                  pl.BlockSpec((1, tm, d), row), vec, pl.BlockSpec((1, d), const), pl.BlockSpec((1, d), const),
                  vec, vec, pl.BlockSpec((d, LANES), const), pl.BlockSpec((1, LANES), const)],
        out_specs=[pl.BlockSpec((1, tm, d), row), pl.BlockSpec((1, tm, d), row),
                   pl.BlockSpec((1, tm, TOP_K), row), pl.BlockSpec((1, tm, TOP_K), row),
                   pl.BlockSpec((SUBLANES, LANES), const)],
        out_shape=[jax.ShapeDtypeStruct((b, t, d), F32), jax.ShapeDtypeStruct((b, t, d), F32),
                   jax.ShapeDtypeStruct((b, t, TOP_K), jnp.int32), jax.ShapeDtypeStruct((b, t, TOP_K), F32),
                   jax.ShapeDtypeStruct((SUBLANES, LANES), F32)],
        compiler_params=_cparams("arbitrary", "arbitrary"),
    )(hy, go, do, w_out, x, g1, gpost, gpre, sc2, sh2, wr, br)


EXPERT_TILE = 256


def _positions_body(topi_ref, base_ref, pos_ref, carry_ref):
    tm = topi_ref.shape[0]

    @pl.when(pl.program_id(0) == 0)
    def _():
        carry_ref[...] = jnp.zeros_like(carry_ref)

    topi = topi_ref[...]
    lane = lax.broadcasted_iota(jnp.int32, (tm, LANES), 1)
    hits = [lane == topi[:, k:k + 1] for k in range(TOP_K)]
    chosen = jnp.zeros((tm, LANES), F32)
    for hit in hits:
        chosen = jnp.where(hit, 1.0, chosen)
    r = lax.broadcasted_iota(jnp.int32, (tm, tm), 0)
    c = lax.broadcasted_iota(jnp.int32, (tm, tm), 1)
    before = jnp.where(c < r, 1.0, 0.0).astype(BF16)
    rank = jnp.dot(before, chosen.astype(BF16), preferred_element_type=F32) + carry_ref[0:1, :] + base_ref[...]
    pos = jnp.zeros((tm, LANES), F32)
    for k, hit in enumerate(hits):
        pos = jnp.where(lane == k, jnp.sum(jnp.where(hit, rank, 0.0), axis=1, keepdims=True), pos)
    pos_ref[...] = pos[:, :TOP_K].astype(jnp.int32)
    carry_ref[...] += jnp.sum(chosen, axis=0, keepdims=True)


def _positions(topi, base, tm):
    n = topi.shape[0]
    return pl.pallas_call(
        _positions_body,
        grid=(n // tm,),
        in_specs=[pl.BlockSpec((tm, TOP_K), lambda i: (i, 0)), pl.BlockSpec((1, LANES), lambda i: (0, 0))],
        out_specs=pl.BlockSpec((tm, TOP_K), lambda i: (i, 0)),
        out_shape=jax.ShapeDtypeStruct((n, TOP_K), jnp.int32),
        scratch_shapes=[pltpu.VMEM((SUBLANES, LANES), F32)],
        compiler_params=_cparams("arbitrary"),
    )(topi, base)


def _row_copy(src, src_row, dst, dst_row, sem):
    return pltpu.make_async_copy(src.at[pl.ds(src_row, 1)], dst.at[pl.ds(dst_row, 1)], sem)


def _scatter_body(seg_ref, pos_ref, h_ref, xs_ref, zero_ref, sem, zsem):
    i = pl.program_id(0)
    ts = pos_ref.shape[1]
    n_exp = seg_ref.shape[1]

    @pl.when(i == 0)
    def _():
        zero_ref[...] = jnp.zeros_like(zero_ref)
        for wait in (False, True):
            def per_expert(e, _):
                def per_row(r, _):
                    cp = _row_copy(zero_ref, 0, xs_ref, seg_ref[0, e] + r, zsem)
                    cp.wait() if wait else cp.start()
                    return 0
                return lax.fori_loop(seg_ref[1, e], seg_ref[2, e], per_row, 0)
            lax.fori_loop(0, n_exp, per_expert, 0)

    for wait in (False, True):
        def per_token(t, _):
            for k in range(TOP_K):
                cp = _row_copy(h_ref, t, xs_ref, pos_ref[k, t], sem)
                cp.wait() if wait else cp.start()
            return 0
        lax.fori_loop(0, ts, per_token, 0)


def _scatter(seg, pos_t, h2, n_rows, ts):
    n, d = h2.shape
    return pl.pallas_call(
        _scatter_body,
        grid_spec=pltpu.PrefetchScalarGridSpec(
            num_scalar_prefetch=1,
            grid=(n // ts,),
            in_specs=[pl.BlockSpec((TOP_K, ts), lambda i, seg: (0, i), memory_space=pltpu.SMEM),
                      pl.BlockSpec((ts, d), lambda i, seg: (i, 0))],
            out_specs=pl.BlockSpec(memory_space=pl.ANY),
            scratch_shapes=[pltpu.VMEM((SUBLANES, d), F32), pltpu.SemaphoreType.DMA(()), pltpu.SemaphoreType.DMA(())],
        ),
        out_shape=jax.ShapeDtypeStruct((n_rows, d), F32),
        compiler_params=_cparams("arbitrary"),
    )(seg, pos_t, h2)


def _expert_body(te_ref, na_ref, x_ref, wgu_ref, bgu_ref, wd_ref, bd_ref, y_ref):
    ff = wd_ref.shape[1]

    @pl.when(pl.program_id(0) < na_ref[0])
    def _():
        gu = jnp.dot(x_ref[...].astype(BF16), wgu_ref[0], preferred_element_type=F32) + bgu_ref[0]
        gate = jnp.minimum(gu[:, :ff], SWIGLU_LIMIT)
        up = jnp.clip(gu[:, ff:], -SWIGLU_LIMIT, SWIGLU_LIMIT)
        act = (up + 1.0) * gate / (1.0 + jnp.exp(-SWIGLU_ALPHA * gate))
        y_ref[...] = jnp.dot(act.astype(BF16), wd_ref[0], preferred_element_type=F32) + bd_ref[0]


def _experts(tile_expert, n_active, xs, wgu, bgu, wd, bd):
    n_rows, d = xs.shape
    n_exp, _, ff2 = wgu.shape
    ff = ff2 // 2
    tile = lambda i, te, na: (jnp.minimum(i, na[0] - 1), 0)
    wsel = lambda i, te, na: (te[jnp.minimum(i, na[0] - 1)], 0, 0)
    return pl.pallas_call(
        _expert_body,
        grid_spec=pltpu.PrefetchScalarGridSpec(
            num_scalar_prefetch=2,
            grid=(n_rows // EXPERT_TILE,),
            in_specs=[pl.BlockSpec((EXPERT_TILE, d), tile),
                      pl.BlockSpec((1, d, ff2), wsel), pl.BlockSpec((1, 1, ff2), wsel),
                      pl.BlockSpec((1, ff, d), wsel), pl.BlockSpec((1, 1, d), wsel)],
            out_specs=pl.BlockSpec((EXPERT_TILE, d), tile),
        ),
        out_shape=jax.ShapeDtypeStruct((n_rows, d), F32),
        compiler_params=_cparams("arbitrary"),
    )(tile_expert, n_active, xs, wgu, bgu.reshape(n_exp, 1, ff2), wd, bd.reshape(n_exp, 1, d))


def _combine_body(pos_ref, w_ref, x_ref, g2_ref, gpost_ref, ys_ref, o_ref, buf, sem):
    tc = w_ref.shape[0]
    for wait in (False, True):
        def per_token(t, _):
            for k in range(TOP_K):
                cp = pltpu.make_async_copy(ys_ref.at[pl.ds(pos_ref[k, t], 1)], buf.at[k, pl.ds(t, 1)], sem)
                cp.wait() if wait else cp.start()
            return 0
        lax.fori_loop(0, tc, per_token, 0)
    w = w_ref[...]
    y = w[:, 0:1] * buf[0]
    for k in range(1, TOP_K):
        y = y + w[:, k:k + 1] * buf[k]
    o_ref[...] = x_ref[...] + g2_ref[0] * (_rms(y) * gpost_ref[...])


def _combine(pos_t, topw, xm, g2, gpost, ys, tokens_per_batch, tc):
    n, d = xm.shape
    per_tile = tokens_per_batch // tc
    tok = lambda i: (i, 0)
    return pl.pallas_call(
        _combine_body,
        grid=(n // tc,),
        in_specs=[pl.BlockSpec((TOP_K, tc), lambda i: (0, i), memory_space=pltpu.SMEM),
                  pl.BlockSpec((tc, TOP_K), tok), pl.BlockSpec((tc, d), tok),
                  pl.BlockSpec((1, 1, d), lambda i: (i // per_tile, 0, 0)), pl.BlockSpec((1, d), lambda i: (0, 0)),
                  pl.BlockSpec(memory_space=pl.ANY)],
        out_specs=pl.BlockSpec((tc, d), tok),
        out_shape=jax.ShapeDtypeStruct((n, d), F32),
        scratch_shapes=[pltpu.VMEM((TOP_K, tc, d), F32), pltpu.SemaphoreType.DMA(())],
        compiler_params=_cparams("arbitrary"),
    )(pos_t, topw, xm, g2, gpost, ys)


def _moe(h2, topi, topw, counts, wgu, bgu, wd, bd, xm, g2, gpost, tm):
    b, t, d = xm.shape
    n_exp = wgu.shape[0]
    n = b * t
    cnt = counts[0, :n_exp].astype(jnp.int32)
    padded = (cnt + EXPERT_TILE - 1) // EXPERT_TILE * EXPERT_TILE
    ends = jnp.cumsum(padded)
    first = ends - padded
    n_rows = n * TOP_K + n_exp * EXPERT_TILE
    tile_start = jnp.arange(n_rows // EXPERT_TILE, dtype=jnp.int32) * EXPERT_TILE
    tile_expert = jnp.minimum(jnp.sum(tile_start[:, None] >= ends[None, :], axis=1), n_exp - 1).astype(jnp.int32)
    n_active = (ends[-1:] // EXPERT_TILE).astype(jnp.int32)
    base = jnp.zeros((1, LANES), F32).at[0, :n_exp].set(first.astype(F32))
    seg = jnp.stack([first, cnt, padded]).astype(jnp.int32)

    pos = _positions(topi.reshape(n, TOP_K), base, tm)
    pos_t = pos.T
    xs = _scatter(seg, pos_t, h2.reshape(n, d), n_rows, tm)
    ys = _experts(tile_expert, n_active, xs, wgu, bgu, wd, bd)
    out = _combine(pos_t, topw.reshape(n, TOP_K), xm.reshape(n, d), g2, gpost, ys, t, min(tm, 256))
    return out.reshape(b, t, d)


def _rope_tables(n_tok, dim, use_pos):
    if not use_pos:
        return jnp.ones((n_tok, LANES), F32), jnp.zeros((n_tok, LANES), F32)
    half = dim // 4
    t = jnp.arange(n_tok, dtype=jnp.int32)
    row = (t // GRID_W).astype(F32)
    col = (t % GRID_W).astype(F32)
    inv_freq = ROPE_THETA ** (-jnp.arange(half, dtype=F32) / half)
    a_row = row[:, None] * inv_freq[None, :]
    a_col = col[:, None] * inv_freq[None, :]
    cos = jnp.concatenate([jnp.cos(a_row)] * 2 + [jnp.cos(a_col)] * 2, axis=1)
    sin = jnp.concatenate([-jnp.sin(a_row), jnp.sin(a_row), -jnp.sin(a_col), jnp.sin(a_col)], axis=1)
    reps = LANES // dim
    return jnp.tile(cos, (1, reps)), jnp.tile(sin, (1, reps))


def _hyena_feats(n_t):
    t = jnp.arange(n_t, dtype=F32)
    bands = jnp.linspace(1e-4, HY_BANDS - 1, HY_BANDS, dtype=F32)
    w = (2.0 * math.pi / n_t) * t
    feats = jnp.concatenate([(t / n_t)[:, None], jnp.cos(w[:, None] * bands), -jnp.sin(w[:, None] * bands)], axis=-1)
    return jnp.pad(feats, ((0, 0), (0, LANES - feats.shape[1])))


def _tile_rows(t, cap):
    return cap if t % cap == 0 else t


def kernel(x, c, ctx, c_ctx, w_ada, b_ada, g_pre_mix, g_post_mix, g_pre_ffn, g_post_ffn, w_in, w_out, hy_conv_w,
           hy_conv_b, hy_w1, hy_b1, hy_w2, hy_b2, hy_w3, hy_freq, hy_decay, hy_bias, q_norm, k_norm, lam_q1, lam_k1,
           lam_q2, lam_k2, subln, w_router, b_router, w_gu, b_gu, w_down, b_down):
    b, s, d = x.shape
    n_ctx = ctx.shape[1]
    depth = w_ada.shape[0]
    n_exp = w_router.shape[-1]

    rows = -(-(b + 1) // SUBLANES) * SUBLANES
    cc = jnp.zeros((rows, d), F32).at[:b].set(c).at[b].set(c_ctx)
    mod = _ada(cc, w_ada, b_ada)

    head_cols = jnp.concatenate([jnp.arange(HEAD_DIM) + HEAD_DIM * h for h in _GQA_HEAD_ORDER])
    in_perm = jnp.concatenate([jnp.arange(Q_START), Q_START + head_cols, jnp.arange(Q_START + GQA_Q_COLS, IN_COLS)])
    out_perm = jnp.concatenate([jnp.arange(HY_WIDTH), HY_WIDTH + head_cols, jnp.arange(HY_WIDTH + GQA_Q_COLS, d)])
    w_in_b = w_in[:, :, in_perm].astype(BF16)
    w_out_b = w_out[:, out_perm, :].astype(BF16)
    w_gu_b = w_gu.astype(BF16)
    w_down_b = w_down.astype(BF16)
    wr_p = jnp.pad(w_router, ((0, 0), (0, 0), (0, LANES - n_exp)))
    br_p = jnp.pad(b_router, ((0, 0), (0, LANES - n_exp)), constant_values=NEG_BIG)[:, None, :]

    lane = jnp.arange(LANES)
    block_diag = (lane[:, None] // HEAD_DIM == lane[None, :] // HEAD_DIM).astype(F32)
    reps = LANES // HEAD_DIM

    tabs_lat = _rope_tables(s, HEAD_DIM, True) + _rope_tables(s, DIFF_QK_DIM, True)
    tabs_ctx = _rope_tables(n_ctx, HEAD_DIM, False) + _rope_tables(n_ctx, DIFF_QK_DIM, False)
    mats_lat = _dft_mats(s)
    feats_lat = _hyena_feats(s)
    if depth > 1:
        mats_ctx = _dft_mats(n_ctx)
        feats_ctx = _hyena_feats(n_ctx)

    tm_lat = _tile_rows(s, 512)
    tm_ctx = _tile_rows(n_ctx, 256)
    lk = n_ctx + s
    kb = 256 if (tm_lat % 256 == 0 and tm_ctx % 256 == 0) else 128

    def filters(i, feats, mats):
        w1p = jnp.pad(hy_w1[i], ((0, LANES - hy_w1.shape[1]), (0, 0)))
        k = _hyena_filters(feats, w1p, hy_b1[i][None], hy_w2[i], hy_b2[i][None], hy_w3[i], hy_freq[i][None],
                           hy_decay[i][None])
        half = mats[0].shape[0] // 2
        return _dft_fwd(mats[0], k, _tile_rows(half, 512))

    xc = ctx
    for i in range(depth):
        ctx_out = i < depth - 1
        lam_init = 0.8 - 0.6 * math.exp(-0.3 * i)
        m_lat = mod[i, :b].reshape(b, 1, 6 * d)
        sh1, sc1, g1, sh2, sc2, g2 = [m_lat[:, :, j * d:(j + 1) * d] for j in range(6)]
        m_ctx = jnp.broadcast_to(mod[i, b].reshape(1, 1, 6 * d), (b, 1, 6 * d))
        csh1, csc1, cg1, csh2, csc2, cg2 = [m_ctx[:, :, j * d:(j + 1) * d] for j in range(6)]

        qg = jnp.tile(q_norm[i], reps)[None] * (HEAD_DIM ** -0.5 * LOG2E)
        kg = jnp.tile(k_norm[i], reps)[None]
        lam_p = jnp.stack([lam_q1[i], lam_k1[i], lam_q2[i], lam_k2[i]])
        proj = functools.partial(_inproj, g=g_pre_mix[i][None], w=w_in_b[i], qg=qg, kg=kg, bd=block_diag, kb=kb)

        hy_c, q_c, k_c, v_c = proj(xc, csc1, csh1, tables=tabs_ctx, tm=tm_ctx)
        hy_l, q_l, k_l, v_l = proj(x, sc1, sh1, tables=tabs_lat, tm=tm_lat)
        k_all = jnp.concatenate([k_c, k_l], axis=1)
        v_all = jnp.concatenate([v_c, v_l], axis=1)

        attn = functools.partial(_attention, lam_p=lam_p, sub=subln[i], lam_init=lam_init)
        go_l, do_l = attn(q_l, k_all, v_all, tq=_tile_rows(s, 256))
        kf_lat = filters(i, feats_lat, mats_lat)
        half_lat = mats_lat[0].shape[0] // 2
        hyo_l = _hyena(hy_l, hy_conv_w[i], hy_conv_b[i][None], hy_bias[i], kf_lat, mats_lat,
                       tm_lat, _tile_rows(half_lat, 512), tm_lat)

        post = functools.partial(_outproj, w_out=w_out_b[i], gpost=g_post_mix[i][None], gpre=g_pre_ffn[i][None],
                                 wr=wr_p[i], br=br_p[i])
        moe = functools.partial(_moe, wgu=w_gu_b[i], bgu=b_gu[i], wd=w_down_b[i], bd=b_down[i],
                                gpost=g_post_ffn[i][None])

        xm, h2, topi, topw, cnt = post(hyo_l, go_l, do_l, x=x, g1=g1, sc2=sc2, sh2=sh2, tm=tm_lat)
        x = moe(h2, topi, topw, cnt, xm=xm, g2=g2, tm=tm_lat)

        if ctx_out:
            go_c, do_c = attn(q_c, k_c, v_c, tq=tm_ctx)
            kf_ctx = filters(i, feats_ctx, mats_ctx)
            half_ctx = mats_ctx[0].shape[0] // 2
            hyo_c = _hyena(hy_c, hy_conv_w[i], hy_conv_b[i][None], hy_bias[i], kf_ctx, mats_ctx,
                           tm_ctx, _tile_rows(half_ctx, 512), tm_ctx)
            xcm, h2c, topic, topwc, cntc = post(hyo_c, go_c, do_c, x=xc, g1=cg1, sc2=csc2, sh2=csh2, tm=tm_ctx)
            xc = moe(h2c, topic, topwc, cntc, xm=xcm, g2=cg2, tm=tm_ctx)
    return x
```

```python
import functools
import math

import jax
import jax.numpy as jnp
from jax import lax
from jax.experimental import pallas as pl
from jax.experimental.pallas import tpu as pltpu

F32 = jnp.float32
BF16 = jnp.bfloat16
HIGHEST = lax.Precision.HIGHEST

HEAD_DIM = 64
ROPE_THETA = 10000.0
NORM_EPS = 1e-6
GRID_W = 64
HY_WIDTH = 256
HY_BANDS = 16
GQA_HEADS = 8
GQA_KV_HEADS = 2
GQA_GROUP = GQA_HEADS // GQA_KV_HEADS
DIFF_HEADS = 4
DIFF_QK_DIM = 32
DIFF_V_DIM = 64
TOP_K = 4
SWIGLU_LIMIT = 7.0
SWIGLU_ALPHA = 1.702

HY_COLS = 3 * HY_WIDTH
GQA_Q_COLS = GQA_HEADS * HEAD_DIM
DIFF_Q_COLS = DIFF_HEADS * 2 * DIFF_QK_DIM
GQA_KV_COLS = GQA_KV_HEADS * HEAD_DIM
DIFF_V_COLS = DIFF_HEADS * DIFF_V_DIM
Q_COLS = GQA_Q_COLS + DIFF_Q_COLS
KV_COLS = GQA_KV_COLS + DIFF_Q_COLS
Q_START = HY_COLS
K_START = Q_START + Q_COLS
IN_COLS = K_START + 2 * KV_COLS
V_HEAD_ROWS = HEAD_DIM + 16
VT_ROWS = (KV_COLS // HEAD_DIM) * V_HEAD_ROWS

LANES = 128
SUBLANES = 8
VMEM_LIMIT_BYTES = 56 * 1024 * 1024
NEG_BIG = -1e30
LOG2E = math.log2(math.e)

_GQA_HEAD_ORDER = tuple(g + GQA_GROUP * j for g in range(GQA_GROUP) for j in range(GQA_KV_HEADS))


def _cparams(*sem):
    return pltpu.CompilerParams(dimension_semantics=sem, vmem_limit_bytes=VMEM_LIMIT_BYTES)


def _rms(x):
    return x * lax.rsqrt(jnp.mean(x * x, axis=-1, keepdims=True) + NORM_EPS)


def _ada_body(c_ref, w_ref, b_ref, o_ref):
    c = c_ref[...]
    s = c / (1.0 + jnp.exp(-c))
    o_ref[0] = jnp.dot(s.astype(BF16), w_ref[0].astype(BF16), preferred_element_type=F32) + b_ref[0]


def _ada(cc, w_ada, b_ada):
    n_layers, d, n = w_ada.shape
    rows = cc.shape[0]
    tn = 512
    return pl.pallas_call(
        _ada_body,
        grid=(n_layers, n // tn),
        in_specs=[pl.BlockSpec((rows, d), lambda l, j: (0, 0)),
                  pl.BlockSpec((1, d, tn), lambda l, j: (l, 0, j)),
                  pl.BlockSpec((1, 1, tn), lambda l, j: (l, 0, j))],
        out_specs=pl.BlockSpec((1, rows, tn), lambda l, j: (l, 0, j)),
        out_shape=jax.ShapeDtypeStruct((n_layers, rows, n), F32),
        compiler_params=_cparams("parallel", "parallel"),
    )(cc, w_ada, b_ada.reshape(n_layers, 1, n))


def _rope(xc, cos, sin, half, lane):
    fwd = pltpu.roll(xc, LANES - half, 1)
    bwd = pltpu.roll(xc, half, 1)
    sw = jnp.where((lane % (2 * half)) < half, fwd, bwd)
    return xc * cos + sw * sin


def _inproj_body(x_ref, sc_ref, sh_ref, g_ref, w_ref, qg_ref, kg_ref, bd_ref,
                 cg_ref, sg_ref, cd_ref, sd_ref, hy_ref, q_ref, k_ref, vt_ref):
    tm = x_ref.shape[1]
    h = _rms(x_ref[0]) * g_ref[...]
    h = h * (1.0 + sc_ref[0]) + sh_ref[0]
    p = jnp.dot(h.astype(BF16), w_ref[...], preferred_element_type=F32)
    hy_ref[0] = p[:, :HY_COLS]

    lane = lax.broadcasted_iota(jnp.int32, (tm, LANES), 1)
    cg, sg, cd, sd = cg_ref[...], sg_ref[...], cd_ref[...], sd_ref[...]
    bd = bd_ref[...]

    def head_norm(xc, gain):
        ms = jnp.dot(xc * xc, bd, precision=HIGHEST, preferred_element_type=F32) * (1.0 / HEAD_DIM)
        return xc * lax.rsqrt(ms + NORM_EPS) * gain

    for c in range(GQA_Q_COLS // LANES):
        xc = p[:, Q_START + c * LANES: Q_START + (c + 1) * LANES]
        xc = head_norm(xc, qg_ref[...])
        q_ref[0, :, c * LANES:(c + 1) * LANES] = _rope(xc, cg, sg, HEAD_DIM // 4, lane).astype(BF16)
    dscale = DIFF_QK_DIM ** -0.5 * LOG2E
    for c in range(DIFF_Q_COLS // LANES):
        o = Q_START + GQA_Q_COLS + c * LANES
        xc = p[:, o:o + LANES] * dscale
        q_ref[0, :, GQA_Q_COLS + c * LANES: GQA_Q_COLS + (c + 1) * LANES] = (
            _rope(xc, cd, sd, DIFF_QK_DIM // 4, lane).astype(BF16))

    gk = head_norm(p[:, K_START:K_START + GQA_KV_COLS], kg_ref[...])
    k_ref[0, :, :GQA_KV_COLS] = _rope(gk, cg, sg, HEAD_DIM // 4, lane).astype(BF16)
    dk0 = K_START + 2 * GQA_KV_COLS
    for c in range(DIFF_Q_COLS // LANES):
        xc = p[:, dk0 + c * LANES: dk0 + (c + 1) * LANES]
        k_ref[0, :, GQA_KV_COLS + c * LANES: GQA_KV_COLS + (c + 1) * LANES] = (
            _rope(xc, cd, sd, DIFF_QK_DIM // 4, lane).astype(BF16))
    kb = vt_ref.shape[3]
    for sblk in range(tm // kb):
        rows = slice(sblk * kb, (sblk + 1) * kb)
        vals = jnp.concatenate([p[rows, K_START + GQA_KV_COLS:K_START + 2 * GQA_KV_COLS],
                                p[rows, dk0 + DIFF_Q_COLS:]], axis=1)
        vals_t = vals.T.astype(BF16)
        ones = jnp.ones((V_HEAD_ROWS - HEAD_DIM, kb), BF16)
        for hd in range(KV_COLS // HEAD_DIM):
            vt_ref[0, sblk, hd * V_HEAD_ROWS:hd * V_HEAD_ROWS + HEAD_DIM, :] = vals_t[hd * HEAD_DIM:(hd + 1) * HEAD_DIM]
            vt_ref[0, sblk, hd * V_HEAD_ROWS + HEAD_DIM:(hd + 1) * V_HEAD_ROWS, :] = ones


def _inproj(x, scale, shift, g, w, qg, kg, bd, tables, tm, kb):
    b, t, d = x.shape
    row = lambda bi, i: (bi, i, 0)
    per_b = lambda bi, i: (bi, 0, 0)
    const = lambda bi, i: (0, 0)
    tab = pl.BlockSpec((tm, LANES), lambda bi, i: (i, 0))
    return pl.pallas_call(
        _inproj_body,
        grid=(b, t // tm),
        in_specs=[pl.BlockSpec((1, tm, d), row),
                  pl.BlockSpec((1, 1, d), per_b), pl.BlockSpec((1, 1, d), per_b),
                  pl.BlockSpec((1, d), const),
                  pl.BlockSpec((d, IN_COLS), const),
                  pl.BlockSpec((1, LANES), const), pl.BlockSpec((1, LANES), const),
                  pl.BlockSpec((LANES, LANES), const),
                  tab, tab, tab, tab],
        out_specs=[pl.BlockSpec((1, tm, HY_COLS), row), pl.BlockSpec((1, tm, Q_COLS), row),
                   pl.BlockSpec((1, tm, KV_COLS), row),
                   pl.BlockSpec((1, tm // kb, VT_ROWS, kb), lambda bi, i: (bi, i, 0, 0))],
        out_shape=[jax.ShapeDtypeStruct((b, t, HY_COLS), F32), jax.ShapeDtypeStruct((b, t, Q_COLS), BF16),
                   jax.ShapeDtypeStruct((b, t, KV_COLS), BF16),
                   jax.ShapeDtypeStruct((b, t // kb, VT_ROWS, kb), BF16)],
        compiler_params=_cparams("parallel", "parallel"),
    )(x, scale, shift, g, w, qg, kg, bd, *tables)


def _filter_body(f_ref, w1_ref, b1_ref, w2_ref, b2_ref, w3_ref, fr_ref, dc_ref, o_ref):
    n = f_ref.shape[0]
    fr = fr_ref[...]
    h = jnp.sin(fr * (jnp.dot(f_ref[...], w1_ref[...], precision=HIGHEST, preferred_element_type=F32) + b1_ref[...]))
    h = jnp.sin(fr * (jnp.dot(h, w2_ref[...], precision=HIGHEST, preferred_element_type=F32) + b2_ref[...]))
    k = jnp.dot(h, w3_ref[...], precision=HIGHEST, preferred_element_type=F32)
    t = lax.broadcasted_iota(jnp.int32, k.shape, 0)
    off = jnp.abs(t - n // 2).astype(F32) * (2.0 / n)
    k = k * jnp.exp(-off * jnp.abs(dc_ref[...]))
    k = k / jnp.sum(jnp.abs(k), axis=0, keepdims=True)
    o_ref[...] = k.astype(BF16)


def _hyena_filters(feats, w1p, b1, w2, b2, w3, freq, decay):
    n = feats.shape[0]
    hid = w2.shape[0]
    cols = w3.shape[1]
    const = lambda j: (0, 0)
    return pl.pallas_call(
        _filter_body,
        grid=(cols // LANES,),
        in_specs=[pl.BlockSpec((n, LANES), const), pl.BlockSpec((LANES, hid), const),
                  pl.BlockSpec((1, hid), const), pl.BlockSpec((hid, hid), const), pl.BlockSpec((1, hid), const),
                  pl.BlockSpec((hid, LANES), lambda j: (0, j)), pl.BlockSpec((1, hid), const),
                  pl.BlockSpec((1, LANES), lambda j: (0, j))],
        out_specs=pl.BlockSpec((n, LANES), lambda j: (0, j)),
        out_shape=jax.ShapeDtypeStruct((n, cols), BF16),
        compiler_params=_cparams("parallel"),
    )(feats, w1p, b1, w2, b2, w3, freq, decay)


def _hyprep_body(p_ref, prev_ref, next_ref, w_ref, b_ref, v_ref, vb_ref, x1_ref, x2_ref, buf):
    i = pl.program_id(1)
    n = pl.num_programs(1)
    tm = p_ref.shape[1]
    buf[SUBLANES:tm + SUBLANES, :] = p_ref[0]
    buf[SUBLANES - 1:SUBLANES, :] = jnp.where(i > 0, prev_ref[0][SUBLANES - 1:SUBLANES, :], 0.0)
    buf[tm + SUBLANES:tm + SUBLANES + 1, :] = jnp.where(i < n - 1, next_ref[0][0:1, :], 0.0)
    w = w_ref[...]
    y = (buf[SUBLANES - 1:tm + SUBLANES - 1, :] * w[0:1] + buf[SUBLANES:tm + SUBLANES, :] * w[1:2]
         + buf[SUBLANES + 1:tm + SUBLANES + 1, :] * w[2:3] + b_ref[...])
    v_ref[...] = y[:, :HY_WIDTH]
    vb_ref[...] = y[:, :HY_WIDTH].astype(BF16)
    x1_ref[...] = y[:, HY_WIDTH:2 * HY_WIDTH]
    x2_ref[...] = y[:, 2 * HY_WIDTH:]


def _hyprep(p_hy, conv_w, conv_b, tm):
    b, t, _ = p_hy.shape
    nb8 = t // SUBLANES
    r8 = tm // SUBLANES
    tmaj = pl.BlockSpec((tm, HY_WIDTH), lambda bi, i: (i, bi))
    shp = lambda dt: jax.ShapeDtypeStruct((t, b * HY_WIDTH), dt)
    return pl.pallas_call(
        _hyprep_body,
        grid=(b, t // tm),
        in_specs=[pl.BlockSpec((1, tm, HY_COLS), lambda bi, i: (bi, i, 0)),
                  pl.BlockSpec((1, SUBLANES, HY_COLS), lambda bi, i: (bi, jnp.maximum(i * r8 - 1, 0), 0)),
                  pl.BlockSpec((1, SUBLANES, HY_COLS), lambda bi, i: (bi, jnp.minimum((i + 1) * r8, nb8 - 1), 0)),
                  pl.BlockSpec((3, HY_COLS), lambda bi, i: (0, 0)),
                  pl.BlockSpec((1, HY_COLS), lambda bi, i: (0, 0))],
        out_specs=[tmaj, tmaj, tmaj, tmaj],
        out_shape=[shp(F32), shp(BF16), shp(F32), shp(F32)],
        scratch_shapes=[pltpu.VMEM((tm + 2 * SUBLANES, HY_COLS), F32)],
        compiler_params=_cparams("parallel", "parallel"),
    )(p_hy, p_hy, p_hy, conv_w, conv_b)


def _dft_mats(n_t):
    n_fft = 3 * n_t // 2
    half = n_fft // 2
    t = jnp.arange(n_t, dtype=jnp.int32)
    f = jnp.arange(half, dtype=jnp.int32)
    step = 2.0 * math.pi / n_fft
    ang = ((f[:, None] * t[None, :]) % n_fft).astype(F32) * step
    alt_t = (1 - 2 * (t % 2)).astype(F32)
    f_re = jnp.cos(ang)
    f_im = jnp.where(f[:, None] == 0, alt_t[None, :], -jnp.sin(ang))
    fwd = jnp.concatenate([f_re, f_im], axis=0).astype(BF16)
    n_out = t + n_t // 2
    ang2 = ((n_out[:, None] * f[None, :]) % n_fft).astype(F32) * step
    alt_n = (1 - 2 * (n_out % 2)).astype(F32)
    g_re = jnp.where(f[None, :] == 0, 1.0, 2.0 * jnp.cos(ang2)).astype(BF16)
    g_im = jnp.where(f[None, :] == 0, alt_n[:, None], -2.0 * jnp.sin(ang2)).astype(BF16)
    return fwd, g_re, g_im


def _dft_fwd_body(fre_ref, fim_ref, u_ref, zre_ref, zim_ref):
    u = u_ref[...]
    zre_ref[...] = jnp.dot(fre_ref[...], u, preferred_element_type=F32)
    zim_ref[...] = jnp.dot(fim_ref[...], u, preferred_element_type=F32)


def _dft_fwd_mul_body(fre_ref, fim_ref, u_ref, kre_ref, kim_ref, zre_ref, zim_ref):
    u = u_ref[...]
    xr = jnp.dot(fre_ref[...], u, preferred_element_type=F32)
    xi = jnp.dot(fim_ref[...], u, preferred_element_type=F32)
    kre, kim = kre_ref[...], kim_ref[...]
    row0 = (lax.broadcasted_iota(jnp.int32, xr.shape, 0) == 0) & (pl.program_id(0) == 0)
    a = xr * kre
    bb = xi * kim
    zre_ref[...] = (a - jnp.where(row0, 0.0, bb)).astype(zre_ref.dtype)
    zim_ref[...] = jnp.where(row0, bb, xr * kim + xi * kre).astype(zim_ref.dtype)


def _dft_fwd(fwd, u, tm, kf=None, order=0):
    n_fft, n_t = fwd.shape
    half = n_fft // 2
    cols = u.shape[1]
    nblk = half // tm
    tn = HY_WIDTH
    in_specs = [pl.BlockSpec((tm, n_t), lambda i, j: (i, 0)),
                pl.BlockSpec((tm, n_t), lambda i, j: (i + nblk, 0)),
                pl.BlockSpec((n_t, tn), lambda i, j: (0, j))]
    args = [fwd, fwd, u]
    if kf is None:
        body, odt = _dft_fwd_body, F32
    else:
        body, odt = _dft_fwd_mul_body, BF16
        in_specs += [pl.BlockSpec((tm, tn), lambda i, j: (i, order)), pl.BlockSpec((tm, tn), lambda i, j: (i, order))]
        args += [kf[0], kf[1]]
    out = pl.BlockSpec((tm, tn), lambda i, j: (i, j))
    return pl.pallas_call(
        body,
        grid=(nblk, cols // tn),
        in_specs=in_specs,
        out_specs=[out, out],
        out_shape=[jax.ShapeDtypeStruct((half, cols), odt)] * 2,
        compiler_params=_cparams("parallel", "parallel"),
    )(*args)


def _dft_inv_body(gre_ref, gim_ref, zre_ref, zim_ref, u_ref, x_ref, bias_ref, *out_refs, inv_n):
    y = (jnp.dot(gre_ref[...], zre_ref[...], preferred_element_type=F32)
         + jnp.dot(gim_ref[...], zim_ref[...], preferred_element_type=F32)) * inv_n
    o = x_ref[...] * (y + u_ref[...] * bias_ref[...])
    if len(out_refs) == 2:
        out_refs[0][...] = o
        out_refs[1][...] = o.astype(BF16)
    else:
        out_refs[0][0] = o.astype(BF16)


def _dft_inv(g_re, g_im, zre, zim, u, xg, bias, tm, batch, last):
    n_t, half = g_re.shape
    cols = u.shape[1]
    tn = HY_WIDTH
    gsp = pl.BlockSpec((tm, half), lambda i, j: (i, 0))
    zsp = pl.BlockSpec((half, tn), lambda i, j: (0, j))
    tsp = pl.BlockSpec((tm, tn), lambda i, j: (i, j))
    if last:
        out_specs = [pl.BlockSpec((1, tm, tn), lambda i, j: (j, i, 0))]
        out_shape = [jax.ShapeDtypeStruct((batch, n_t, tn), BF16)]
    else:
        out_specs = [tsp, tsp]
        out_shape = [jax.ShapeDtypeStruct((n_t, cols), F32), jax.ShapeDtypeStruct((n_t, cols), BF16)]
    return pl.pallas_call(
        functools.partial(_dft_inv_body, inv_n=1.0 / (2 * half)),
        grid=(n_t // tm, cols // tn),
        in_specs=[gsp, gsp, zsp, zsp, tsp, tsp, pl.BlockSpec((1, tn), lambda i, j: (0, 0))],
        out_specs=out_specs,
        out_shape=out_shape,
        compiler_params=_cparams("parallel", "parallel"),
    )(g_re, g_im, zre, zim, u, xg, bias)


def _hyena(p_hy, conv_w, conv_b, hy_bias, kf, mats, tm_prep, tm_f, tm_t):
    b = p_hy.shape[0]
    fwd, g_re, g_im = mats
    v, vb, x1, x2 = _hyprep(p_hy, conv_w, conv_b, tm_prep)
    zre, zim = _dft_fwd(fwd, vb, tm_f, kf, 0)
    y1, y1b = _dft_inv(g_re, g_im, zre, zim, v, x1, hy_bias[0:1], tm_t, b, False)
    zre, zim = _dft_fwd(fwd, y1b, tm_f, kf, 1)
    (hy,) = _dft_inv(g_re, g_im, zre, zim, y1, x2, hy_bias[1:2], tm_t, b, True)
    return hy


def _attn_body(q_ref, k_ref, vt_ref, lam_ref, sub_ref, go_ref, do_ref, *, lam_init, group, ahead):
    tq = q_ref.shape[1]
    nkb, _, kb = vt_ref.shape[1:]
    lp = lam_ref[...]
    lam = (jnp.exp(jnp.sum(lp[0:1] * lp[1:2], axis=1, keepdims=True))
           - jnp.exp(jnp.sum(lp[2:3] * lp[3:4], axis=1, keepdims=True)) + lam_init)

    def attend(maps):
        def score(i, qm, kcol, kw):
            off = pl.multiple_of(i * kb, kb)
            return lax.dot_general(k_ref[0, pl.ds(off, kb), kcol:kcol + kw], qm, (((1,), (1,)), ((), ())),
                                   preferred_element_type=F32)

        def update(i, s, m, acc, vrow):
            vblk = vt_ref[0, i, vrow:vrow + V_HEAD_ROWS, :]
            m_new = jnp.maximum(m, jnp.max(s, axis=0, keepdims=True))
            alpha = jnp.exp2(m - m_new)
            p = jnp.exp2((s - m_new).astype(BF16))
            return m_new, alpha * acc + jnp.dot(vblk, p, preferred_element_type=F32)

        def body(i, state):
            scores, new = {}, []
            for j in range(len(maps) + ahead):
                if j < len(maps):
                    scores[j] = score(i, *maps[j][:3])
                if j >= ahead:
                    jj = j - ahead
                    new.append(update(i, scores.pop(jj), *state[jj], maps[jj][3]))
            return tuple(new)

        init = tuple((jnp.full((1, tq), NEG_BIG, F32), jnp.zeros((V_HEAD_ROWS, tq), F32)) for _ in maps)
        return [acc[:HEAD_DIM] / acc[HEAD_DIM:HEAD_DIM + 1] for (_, acc) in lax.fori_loop(0, nkb, body, init)]

    lane = lax.broadcasted_iota(jnp.int32, (tq, LANES), 1)
    lane_d = lax.broadcasted_iota(jnp.int32, (tq, DIFF_Q_COLS), 1)
    dq = q_ref[0, :, GQA_Q_COLS:]
    maps = []
    for c in range(GQA_Q_COLS // LANES):
        qc = q_ref[0, :, c * LANES:(c + 1) * LANES]
        for j in range(GQA_KV_HEADS):
            maps.append((jnp.where((lane // HEAD_DIM) == j, qc, jnp.zeros_like(qc)), 0, GQA_KV_COLS, j * V_HEAD_ROWS))
    for h in range(DIFF_HEADS):
        for cmap in range(2):
            qm = jnp.where((lane_d // DIFF_QK_DIM) == 2 * h + cmap, dq, jnp.zeros_like(dq))
            maps.append((qm, GQA_KV_COLS, DIFF_Q_COLS, (GQA_KV_HEADS + h) * V_HEAD_ROWS))
    outs = []
    for g in range(0, len(maps), group):
        outs += attend(maps[g:g + group])

    n_gqa = GQA_HEADS
    for c in range(GQA_Q_COLS // LANES):
        pair = jnp.concatenate(outs[GQA_KV_HEADS * c:GQA_KV_HEADS * (c + 1)], axis=0)
        go_ref[0, :, c * LANES:(c + 1) * LANES] = pair.T.astype(BF16)
    sub = sub_ref[...] * (1.0 - lam_init)
    heads = []
    for h in range(DIFF_HEADS):
        d = outs[n_gqa + 2 * h] - lam * outs[n_gqa + 2 * h + 1]
        ms = jnp.mean(d * d, axis=0, keepdims=True)
        heads.append(d * lax.rsqrt(ms + NORM_EPS) * sub)
    do_ref[0] = jnp.concatenate(heads, axis=0).T.astype(BF16)


def _attention(q, k, vt, lam_p, sub, lam_init, tq, group=16, ahead=10):
    b, t, _ = q.shape
    lk = k.shape[1]
    row = lambda bi, i: (bi, i, 0)
    sub_t = jnp.broadcast_to(sub.reshape(DIFF_V_DIM, 1), (DIFF_V_DIM, tq))
    return pl.pallas_call(
        functools.partial(_attn_body, lam_init=lam_init, group=group, ahead=ahead),
        grid=(b, t // tq),
        in_specs=[pl.BlockSpec((1, tq, Q_COLS), row),
                  pl.BlockSpec((1, lk, KV_COLS), lambda bi, i: (bi, 0, 0)),
                  pl.BlockSpec((1,) + vt.shape[1:], lambda bi, i: (bi, 0, 0, 0)),
                  pl.BlockSpec(lam_p.shape, lambda bi, i: (0, 0)),
                  pl.BlockSpec((DIFF_V_DIM, tq), lambda bi, i: (0, 0))],
        out_specs=[pl.BlockSpec((1, tq, GQA_Q_COLS), row), pl.BlockSpec((1, tq, DIFF_V_COLS), row)],
        out_shape=[jax.ShapeDtypeStruct((b, t, GQA_Q_COLS), BF16), jax.ShapeDtypeStruct((b, t, DIFF_V_COLS), BF16)],
        compiler_params=_cparams("parallel", "parallel"),
    )(q, k, vt, lam_p, sub_t)


def _outproj_body(hy_ref, go_ref, do_ref, w_ref, x_ref, g1_ref, gpost_ref, gpre_ref, sc_ref, sh_ref,
                  wr_ref, br_ref, xm_ref, h2_ref, topi_ref, topw_ref, cnt_ref):
    y = (jnp.dot(hy_ref[0], w_ref[:HY_WIDTH], preferred_element_type=F32)
         + jnp.dot(go_ref[0], w_ref[HY_WIDTH:HY_WIDTH + GQA_Q_COLS], preferred_element_type=F32)
         + jnp.dot(do_ref[0], w_ref[HY_WIDTH + GQA_Q_COLS:], preferred_element_type=F32))
    xm = x_ref[0] + g1_ref[0] * (_rms(y) * gpost_ref[...])
    xm_ref[0] = xm
    h2 = _rms(xm) * gpre_ref[...] * (1.0 + sc_ref[0]) + sh_ref[0]
    h2_ref[0] = h2

    logits = jnp.dot(h2, wr_ref[...], precision=HIGHEST, preferred_element_type=F32) + br_ref[...]
    lane = lax.broadcasted_iota(jnp.int32, logits.shape, 1)
    topi = jnp.zeros(logits.shape, jnp.int32)
    topw = jnp.zeros_like(logits)
    chosen = jnp.zeros_like(logits)
    denom = jnp.zeros((logits.shape[0], 1), F32)
    top = None
    for k in range(TOP_K):
        mx = jnp.max(logits, axis=1, keepdims=True)
        first = jnp.min(jnp.where(logits == mx, lane, LANES), axis=1, keepdims=True)
        sel = lane == first
        top = mx if top is None else top
        e = jnp.exp(mx - top)
        topi = jnp.where(lane == k, first, topi)
        topw = jnp.where(lane == k, e, topw)
        chosen = jnp.where(sel, 1.0, chosen)
        denom = denom + e
        logits = jnp.where(sel, NEG_BIG, logits)
    topi_ref[0] = topi[:, :TOP_K]
    topw_ref[0] = (topw / denom)[:, :TOP_K]

    @pl.when((pl.program_id(0) == 0) & (pl.program_id(1) == 0))
    def _():
        cnt_ref[...] = jnp.zeros_like(cnt_ref)
    cnt_ref[...] += jnp.sum(chosen, axis=0, keepdims=True)


def _outproj(hy, go, do, w_out, x, g1, gpost, gpre, sc2, sh2, wr, br, tm):
    b, t, d = x.shape
    row = lambda bi, i: (bi, i, 0)
    per_b = lambda bi, i: (bi, 0, 0)
    const = lambda bi, i: (0, 0)
    vec = pl.BlockSpec((1, 1, d), per_b)
    return pl.pallas_call(
        _outproj_body,
        grid=(b, t // tm),
        in_specs=[pl.BlockSpec((1, tm, HY_WIDTH), row), pl.BlockSpec((1, tm, GQA_Q_COLS), row),
                  pl.BlockSpec((1, tm, DIFF_V_COLS), row), pl.BlockSpec(w_out.shape, const),
                  pl.BlockSpec((1, tm, d), row), vec, pl.BlockSpec((1, d), const), pl.BlockSpec((1, d), const),
                  vec, vec, pl.BlockSpec((d, LANES), const), pl.BlockSpec((1, LANES), const)],
        out_specs=[pl.BlockSpec((1, tm, d), row), pl.BlockSpec((1, tm, d), row),
                   pl.BlockSpec((1, tm, TOP_K), row), pl.BlockSpec((1, tm, TOP_K), row),
                   pl.BlockSpec((SUBLANES, LANES), const)],
        out_shape=[jax.ShapeDtypeStruct((b, t, d), F32), jax.ShapeDtypeStruct((b, t, d), F32),
                   jax.ShapeDtypeStruct((b, t, TOP_K), jnp.int32), jax.ShapeDtypeStruct((b, t, TOP_K), F32),
                   jax.ShapeDtypeStruct((SUBLANES, LANES), F32)],
        compiler_params=_cparams("arbitrary", "arbitrary"),
    )(hy, go, do, w_out, x, g1, gpost, gpre, sc2, sh2, wr, br)


EXPERT_TILE = 256


def _positions_body(topi_ref, base_ref, pos_ref, carry_ref):
    tm = topi_ref.shape[0]

    @pl.when(pl.program_id(0) == 0)
    def _():
        carry_ref[...] = jnp.zeros_like(carry_ref)

    topi = topi_ref[...]
    lane = lax.broadcasted_iota(jnp.int32, (tm, LANES), 1)
    hits = [lane == topi[:, k:k + 1] for k in range(TOP_K)]
    chosen = jnp.zeros((tm, LANES), F32)
    for hit in hits:
        chosen = jnp.where(hit, 1.0, chosen)
    r = lax.broadcasted_iota(jnp.int32, (tm, tm), 0)
    c = lax.broadcasted_iota(jnp.int32, (tm, tm), 1)
    before = jnp.where(c < r, 1.0, 0.0).astype(BF16)
    rank = jnp.dot(before, chosen.astype(BF16), preferred_element_type=F32) + carry_ref[0:1, :] + base_ref[...]
    pos = jnp.zeros((tm, LANES), F32)
    for k, hit in enumerate(hits):
        pos = jnp.where(lane == k, jnp.sum(jnp.where(hit, rank, 0.0), axis=1, keepdims=True), pos)
    pos_ref[...] = pos[:, :TOP_K].astype(jnp.int32)
    carry_ref[...] += jnp.sum(chosen, axis=0, keepdims=True)


def _positions(topi, base, tm):
    n = topi.shape[0]
    return pl.pallas_call(
        _positions_body,
        grid=(n // tm,),
        in_specs=[pl.BlockSpec((tm, TOP_K), lambda i: (i, 0)), pl.BlockSpec((1, LANES), lambda i: (0, 0))],
        out_specs=pl.BlockSpec((tm, TOP_K), lambda i: (i, 0)),
        out_shape=jax.ShapeDtypeStruct((n, TOP_K), jnp.int32),
        scratch_shapes=[pltpu.VMEM((SUBLANES, LANES), F32)],
        compiler_params=_cparams("arbitrary"),
    )(topi, base)


def _row_copy(src, src_row, dst, dst_row, sem):
    return pltpu.make_async_copy(src.at[pl.ds(src_row, 1)], dst.at[pl.ds(dst_row, 1)], sem)


def _scatter_body(seg_ref, pos_ref, h_ref, xs_ref, zero_ref, sem, zsem):
    i = pl.program_id(0)
    n_groups = h_ref.shape[0]
    n_exp = seg_ref.shape[1]

    @pl.when(i == 0)
    def _():
        zero_ref[...] = jnp.zeros_like(zero_ref)
        for wait in (False, True):
            def per_expert(e, _):
                def per_row(r, _):
                    cp = _row_copy(zero_ref, 0, xs_ref, seg_ref[0, e] + r, zsem)
                    cp.wait() if wait else cp.start()
                    return 0
                return lax.fori_loop(seg_ref[1, e], seg_ref[2, e], per_row, 0)
            lax.fori_loop(0, n_exp, per_expert, 0)

    for wait in (False, True):
        def per_group(g, _):
            p0 = g * (SUBLANES * TOP_K)
            for j in range(SUBLANES):
                for k in range(TOP_K):
                    cp = pltpu.make_async_copy(h_ref.at[g, pl.ds(j, 1)],
                                               xs_ref.at[pl.ds(pos_ref[p0 + j * TOP_K + k], 1)], sem)
                    cp.wait() if wait else cp.start(priority=k % 2)
            return 0
        lax.fori_loop(0, n_groups, per_group, 0)


def _scatter(seg, pos_flat, h2, n_rows, ts):
    n, d = h2.shape
    h2 = h2.reshape(n // SUBLANES, SUBLANES, d)
    return pl.pallas_call(
        _scatter_body,
        grid_spec=pltpu.PrefetchScalarGridSpec(
            num_scalar_prefetch=1,
            grid=(n // ts,),
            in_specs=[pl.BlockSpec((ts * TOP_K,), lambda i, seg: (i,), memory_space=pltpu.SMEM),
                      pl.BlockSpec((ts // SUBLANES, SUBLANES, d), lambda i, seg: (i, 0, 0))],
            out_specs=pl.BlockSpec(memory_space=pl.ANY),
            scratch_shapes=[pltpu.VMEM((SUBLANES, d), F32), pltpu.SemaphoreType.DMA(()), pltpu.SemaphoreType.DMA(())],
        ),
        out_shape=jax.ShapeDtypeStruct((n_rows, d), F32),
        compiler_params=_cparams("arbitrary"),
    )(seg, pos_flat, h2)


def _expert_body(te_ref, na_ref, x_ref, wgu_ref, bgu_ref, wd_ref, bd_ref, y_ref, wgu_b, wd_b):
    i = pl.program_id(0)
    ff = wd_ref.shape[1]
    active = i < na_ref[0]
    new_expert = (i == 0) | (te_ref[i] != te_ref[jnp.maximum(i - 1, 0)])

    @pl.when(active & new_expert)
    def _():
        wgu_b[...] = wgu_ref[0].astype(BF16)
        wd_b[...] = wd_ref[0].astype(BF16)

    @pl.when(active)
    def _():
        gu = jnp.dot(x_ref[...].astype(BF16), wgu_b[...], preferred_element_type=F32) + bgu_ref[0]
        gate = jnp.minimum(gu[:, :ff], SWIGLU_LIMIT)
        up = jnp.clip(gu[:, ff:], -SWIGLU_LIMIT, SWIGLU_LIMIT)
        act = (up + 1.0) * gate / (1.0 + jnp.exp(-SWIGLU_ALPHA * gate))
        y_ref[...] = jnp.dot(act.astype(BF16), wd_b[...], preferred_element_type=F32) + bd_ref[0]


def _experts(tile_expert, n_active, xs, wgu, bgu, wd, bd, layer):
    n_rows, d = xs.shape
    depth, n_exp, _, ff2 = wgu.shape
    ff = ff2 // 2
    tile = lambda i, te, na: (jnp.minimum(i, na[0] - 1), 0)
    first_expert = layer * n_exp
    wsel = lambda i, te, na: (first_expert + te[jnp.minimum(i, na[0] - 1)], 0, 0)
    n_exp = depth * n_exp
    wgu = wgu.reshape(n_exp, d, ff2)
    wd = wd.reshape(n_exp, ff, d)
    return pl.pallas_call(
        _expert_body,
        grid_spec=pltpu.PrefetchScalarGridSpec(
            num_scalar_prefetch=2,
            grid=(n_rows // EXPERT_TILE,),
            in_specs=[pl.BlockSpec((EXPERT_TILE, d), tile),
                      pl.BlockSpec((1, d, ff2), wsel), pl.BlockSpec((1, 1, ff2), wsel),
                      pl.BlockSpec((1, ff, d), wsel), pl.BlockSpec((1, 1, d), wsel)],
            out_specs=pl.BlockSpec((EXPERT_TILE, d), tile),
            scratch_shapes=[pltpu.VMEM((d, ff2), BF16), pltpu.VMEM((ff, d), BF16)],
        ),
        out_shape=jax.ShapeDtypeStruct((n_rows, d), F32),
        compiler_params=_cparams("arbitrary"),
    )(tile_expert, n_active, xs, wgu, bgu.reshape(n_exp, 1, ff2), wd, bd.reshape(n_exp, 1, d))


def _combine_body(pos_ref, w_ref, x_ref, g2_ref, gpost_ref, ys_ref, o_ref, buf, sem):
    tc, d = x_ref.shape
    for wait in (False, True):
        def per_group(g, _):
            p0 = g * (SUBLANES * TOP_K)
            for j in range(SUBLANES):
                for k in range(TOP_K):
                    cp = pltpu.make_async_copy(ys_ref.at[pl.ds(pos_ref[p0 + j * TOP_K + k], 1)],
                                               buf.at[k, g, pl.ds(j, 1)], sem)
                    cp.wait() if wait else cp.start(priority=k % 2)
            return 0
        lax.fori_loop(0, tc // SUBLANES, per_group, 0)
    w = w_ref[...]
    y = w[:, 0:1] * buf[0].reshape(tc, d)
    for k in range(1, TOP_K):
        y = y + w[:, k:k + 1] * buf[k].reshape(tc, d)
    o_ref[...] = x_ref[...] + g2_ref[0] * (_rms(y) * gpost_ref[...])


def _combine(pos_flat, topw, xm, g2, gpost, ys, tokens_per_batch, tc):
    n, d = xm.shape
    per_tile = tokens_per_batch // tc
    tok = lambda i: (i, 0)
    return pl.pallas_call(
        _combine_body,
        grid=(n // tc,),
        in_specs=[pl.BlockSpec((tc * TOP_K,), lambda i: (i,), memory_space=pltpu.SMEM),
                  pl.BlockSpec((tc, TOP_K), tok), pl.BlockSpec((tc, d), tok),
                  pl.BlockSpec((1, 1, d), lambda i: (i // per_tile, 0, 0)), pl.BlockSpec((1, d), lambda i: (0, 0)),
                  pl.BlockSpec(memory_space=pl.ANY)],
        out_specs=pl.BlockSpec((tc, d), tok),
        out_shape=jax.ShapeDtypeStruct((n, d), F32),
        scratch_shapes=[pltpu.VMEM((TOP_K, tc // SUBLANES, SUBLANES, d), F32), pltpu.SemaphoreType.DMA(())],
        compiler_params=_cparams("arbitrary"),
    )(pos_flat, topw, xm, g2, gpost, ys)


def _moe(h2, topi, topw, counts, wgu, bgu, wd, bd, layer, xm, g2, gpost, tm):
    b, t, d = xm.shape
    n_exp = wgu.shape[1]
    n = b * t
    cnt = counts[0, :n_exp].astype(jnp.int32)
    padded = (cnt + EXPERT_TILE - 1) // EXPERT_TILE * EXPERT_TILE
    ends = jnp.cumsum(padded)
    first = ends - padded
    n_rows = n * TOP_K + n_exp * EXPERT_TILE
    tile_start = jnp.arange(n_rows // EXPERT_TILE, dtype=jnp.int32) * EXPERT_TILE
    tile_expert = jnp.minimum(jnp.sum(tile_start[:, None] >= ends[None, :], axis=1), n_exp - 1).astype(jnp.int32)
    n_active = (ends[-1:] // EXPERT_TILE).astype(jnp.int32)
    base = jnp.zeros((1, LANES), F32).at[0, :n_exp].set(first.astype(F32))
    seg = jnp.stack([first, cnt, padded]).astype(jnp.int32)

    pos = _positions(topi.reshape(n, TOP_K), base, tm)
    pos_flat = pos.reshape(n * TOP_K)
    xs = _scatter(seg, pos_flat, h2.reshape(n, d), n_rows, tm)
    ys = _experts(tile_expert, n_active, xs, wgu, bgu, wd, bd, layer)
    out = _combine(pos_flat, topw.reshape(n, TOP_K), xm.reshape(n, d), g2, gpost, ys, t, min(tm, 256))
    return out.reshape(b, t, d)


def _rope_tables(n_tok, dim, use_pos):
    if not use_pos:
        return jnp.ones((n_tok, LANES), F32), jnp.zeros((n_tok, LANES), F32)
    half = dim // 4
    t = jnp.arange(n_tok, dtype=jnp.int32)
    row = (t // GRID_W).astype(F32)
    col = (t % GRID_W).astype(F32)
    inv_freq = ROPE_THETA ** (-jnp.arange(half, dtype=F32) / half)
    a_row = row[:, None] * inv_freq[None, :]
    a_col = col[:, None] * inv_freq[None, :]
    cos = jnp.concatenate([jnp.cos(a_row)] * 2 + [jnp.cos(a_col)] * 2, axis=1)
    sin = jnp.concatenate([-jnp.sin(a_row), jnp.sin(a_row), -jnp.sin(a_col), jnp.sin(a_col)], axis=1)
    reps = LANES // dim
    return jnp.tile(cos, (1, reps)), jnp.tile(sin, (1, reps))


def _hyena_feats(n_t):
    t = jnp.arange(n_t, dtype=F32)
    bands = jnp.linspace(1e-4, HY_BANDS - 1, HY_BANDS, dtype=F32)
    w = (2.0 * math.pi / n_t) * t
    feats = jnp.concatenate([(t / n_t)[:, None], jnp.cos(w[:, None] * bands), -jnp.sin(w[:, None] * bands)], axis=-1)
    return jnp.pad(feats, ((0, 0), (0, LANES - feats.shape[1])))


def _tile_rows(t, cap):
    return cap if t % cap == 0 else t


def kernel(x, c, ctx, c_ctx, w_ada, b_ada, g_pre_mix, g_post_mix, g_pre_ffn, g_post_ffn, w_in, w_out, hy_conv_w,
           hy_conv_b, hy_w1, hy_b1, hy_w2, hy_b2, hy_w3, hy_freq, hy_decay, hy_bias, q_norm, k_norm, lam_q1, lam_k1,
           lam_q2, lam_k2, subln, w_router, b_router, w_gu, b_gu, w_down, b_down):
    b, s, d = x.shape
    n_ctx = ctx.shape[1]
    depth = w_ada.shape[0]
    n_exp = w_router.shape[-1]

    rows = -(-(b + 1) // SUBLANES) * SUBLANES
    cc = jnp.zeros((rows, d), F32).at[:b].set(c).at[b].set(c_ctx)
    mod = _ada(cc, w_ada, b_ada)

    head_cols = jnp.concatenate([jnp.arange(HEAD_DIM) + HEAD_DIM * h for h in _GQA_HEAD_ORDER])
    in_perm = jnp.concatenate([jnp.arange(Q_START), Q_START + head_cols, jnp.arange(Q_START + GQA_Q_COLS, IN_COLS)])
    out_perm = jnp.concatenate([jnp.arange(HY_WIDTH), HY_WIDTH + head_cols, jnp.arange(HY_WIDTH + GQA_Q_COLS, d)])
    w_in_b = w_in[:, :, in_perm].astype(BF16)
    w_out_b = w_out[:, out_perm, :].astype(BF16)
    wr_p = jnp.pad(w_router, ((0, 0), (0, 0), (0, LANES - n_exp)))
    br_p = jnp.pad(b_router, ((0, 0), (0, LANES - n_exp)), constant_values=NEG_BIG)[:, None, :]

    lane = jnp.arange(LANES)
    block_diag = (lane[:, None] // HEAD_DIM == lane[None, :] // HEAD_DIM).astype(F32)
    reps = LANES // HEAD_DIM

    tabs_lat = _rope_tables(s, HEAD_DIM, True) + _rope_tables(s, DIFF_QK_DIM, True)
    tabs_ctx = _rope_tables(n_ctx, HEAD_DIM, False) + _rope_tables(n_ctx, DIFF_QK_DIM, False)
    mats_lat = _dft_mats(s)
    feats_lat = _hyena_feats(s)
    if depth > 1:
        mats_ctx = _dft_mats(n_ctx)
        feats_ctx = _hyena_feats(n_ctx)

    tm_lat = _tile_rows(s, 512)
    tm_ctx = _tile_rows(n_ctx, 256)
    lk = n_ctx + s
    kb = 256 if (tm_lat % 256 == 0 and tm_ctx % 256 == 0) else 128

    def filters(i, feats, mats):
        w1p = jnp.pad(hy_w1[i], ((0, LANES - hy_w1.shape[1]), (0, 0)))
        k = _hyena_filters(feats, w1p, hy_b1[i][None], hy_w2[i], hy_b2[i][None], hy_w3[i], hy_freq[i][None],
                           hy_decay[i][None])
        half = mats[0].shape[0] // 2
        return _dft_fwd(mats[0], k, _tile_rows(half, 512))

    xc = ctx
    for i in range(depth):
        ctx_out = i < depth - 1
        lam_init = 0.8 - 0.6 * math.exp(-0.3 * i)
        m_lat = mod[i, :b].reshape(b, 1, 6 * d)
        sh1, sc1, g1, sh2, sc2, g2 = [m_lat[:, :, j * d:(j + 1) * d] for j in range(6)]
        m_ctx = jnp.broadcast_to(mod[i, b].reshape(1, 1, 6 * d), (b, 1, 6 * d))
        csh1, csc1, cg1, csh2, csc2, cg2 = [m_ctx[:, :, j * d:(j + 1) * d] for j in range(6)]

        qg = jnp.tile(q_norm[i], reps)[None] * (HEAD_DIM ** -0.5 * LOG2E)
        kg = jnp.tile(k_norm[i], reps)[None]
        lam_p = jnp.stack([lam_q1[i], lam_k1[i], lam_q2[i], lam_k2[i]])
        proj = functools.partial(_inproj, g=g_pre_mix[i][None], w=w_in_b[i], qg=qg, kg=kg, bd=block_diag, kb=kb)

        hy_c, q_c, k_c, v_c = proj(xc, csc1, csh1, tables=tabs_ctx, tm=tm_ctx)
        hy_l, q_l, k_l, v_l = proj(x, sc1, sh1, tables=tabs_lat, tm=tm_lat)
        k_all = jnp.concatenate([k_c, k_l], axis=1)
        v_all = jnp.concatenate([v_c, v_l], axis=1)

        attn = functools.partial(_attention, lam_p=lam_p, sub=subln[i], lam_init=lam_init)
        go_l, do_l = attn(q_l, k_all, v_all, tq=_tile_rows(s, 256))
        kf_lat = filters(i, feats_lat, mats_lat)
        half_lat = mats_lat[0].shape[0] // 2
        hyo_l = _hyena(hy_l, hy_conv_w[i], hy_conv_b[i][None], hy_bias[i], kf_lat, mats_lat,
                       tm_lat, _tile_rows(half_lat, 512), tm_lat)

        post = functools.partial(_outproj, w_out=w_out_b[i], gpost=g_post_mix[i][None], gpre=g_pre_ffn[i][None],
                                 wr=wr_p[i], br=br_p[i])
        moe = functools.partial(_moe, wgu=w_gu, bgu=b_gu, wd=w_down, bd=b_down, layer=i,
                                gpost=g_post_ffn[i][None])

        xm, h2, topi, topw, cnt = post(hyo_l, go_l, do_l, x=x, g1=g1, sc2=sc2, sh2=sh2, tm=tm_lat)
        x = moe(h2, topi, topw, cnt, xm=xm, g2=g2, tm=tm_lat)

        if ctx_out:
            go_c, do_c = attn(q_c, k_c, v_c, tq=tm_ctx)
            kf_ctx = filters(i, feats_ctx, mats_ctx)
            half_ctx = mats_ctx[0].shape[0] // 2
            hyo_c = _hyena(hy_c, hy_conv_w[i], hy_conv_b[i][None], hy_bias[i], kf_ctx, mats_ctx,
                           tm_ctx, _tile_rows(half_ctx, 512), tm_ctx)
            xcm, h2c, topic, topwc, cntc = post(hyo_c, go_c, do_c, x=xc, g1=cg1, sc2=csc2, sh2=csh2, tm=tm_ctx)
            xc = moe(h2c, topic, topwc, cntc, xm=xcm, g2=cg2, tm=tm_ctx)
    return x
```

```python
import functools
import math

import jax
import jax.numpy as jnp
from jax import lax
from jax.experimental import pallas as pl
from jax.experimental.pallas import tpu as pltpu

F32 = jnp.float32
BF16 = jnp.bfloat16
HIGHEST = lax.Precision.HIGHEST

HEAD_DIM = 64
ROPE_THETA = 10000.0
NORM_EPS = 1e-6
GRID_W = 64
HY_WIDTH = 256
HY_BANDS = 16
GQA_HEADS = 8
GQA_KV_HEADS = 2
GQA_GROUP = GQA_HEADS // GQA_KV_HEADS
DIFF_HEADS = 4
DIFF_QK_DIM = 32
DIFF_V_DIM = 64
TOP_K = 4
SWIGLU_LIMIT = 7.0
SWIGLU_ALPHA = 1.702

HY_COLS = 3 * HY_WIDTH
GQA_Q_COLS = GQA_HEADS * HEAD_DIM
DIFF_Q_COLS = DIFF_HEADS * 2 * DIFF_QK_DIM
GQA_KV_COLS = GQA_KV_HEADS * HEAD_DIM
DIFF_V_COLS = DIFF_HEADS * DIFF_V_DIM
Q_COLS = GQA_Q_COLS + DIFF_Q_COLS
KV_COLS = GQA_KV_COLS + DIFF_Q_COLS
Q_START = HY_COLS
K_START = Q_START + Q_COLS
IN_COLS = K_START + 2 * KV_COLS
V_HEAD_ROWS = HEAD_DIM + 16
VT_ROWS = (KV_COLS // HEAD_DIM) * V_HEAD_ROWS

LANES = 128
SUBLANES = 8
VMEM_LIMIT_BYTES = 56 * 1024 * 1024
NEG_BIG = -1e30
LOG2E = math.log2(math.e)

_GQA_HEAD_ORDER = tuple(g + GQA_GROUP * j for g in range(GQA_GROUP) for j in range(GQA_KV_HEADS))


def _cparams(*sem):
    return pltpu.CompilerParams(dimension_semantics=sem, vmem_limit_bytes=VMEM_LIMIT_BYTES)


def _rms(x):
    return x * lax.rsqrt(jnp.mean(x * x, axis=-1, keepdims=True) + NORM_EPS)


def _ada_body(c_ref, w_ref, b_ref, o_ref):
    c = c_ref[...]
    s = c / (1.0 + jnp.exp(-c))
    o_ref[0] = jnp.dot(s.astype(BF16), w_ref[0].astype(BF16), preferred_element_type=F32) + b_ref[0]


def _ada(cc, w_ada, b_ada):
    n_layers, d, n = w_ada.shape
    rows = cc.shape[0]
    tn = 512
    return pl.pallas_call(
        _ada_body,
        grid=(n_layers, n // tn),
        in_specs=[pl.BlockSpec((rows, d), lambda l, j: (0, 0)),
                  pl.BlockSpec((1, d, tn), lambda l, j: (l, 0, j)),
                  pl.BlockSpec((1, 1, tn), lambda l, j: (l, 0, j))],
        out_specs=pl.BlockSpec((1, rows, tn), lambda l, j: (l, 0, j)),
        out_shape=jax.ShapeDtypeStruct((n_layers, rows, n), F32),
        compiler_params=_cparams("parallel", "parallel"),
    )(cc, w_ada, b_ada.reshape(n_layers, 1, n))


def _rope(xc, cos, sin, half, lane):
    fwd = pltpu.roll(xc, LANES - half, 1)
    bwd = pltpu.roll(xc, half, 1)
    sw = jnp.where((lane % (2 * half)) < half, fwd, bwd)
    return xc * cos + sw * sin


def _inproj_body(x_ref, sc_ref, sh_ref, g_ref, w_ref, qg_ref, kg_ref, bd_ref,
                 cg_ref, sg_ref, cd_ref, sd_ref, hy_ref, q_ref, k_ref, vt_ref):
    tm = x_ref.shape[1]
    h = _rms(x_ref[0]) * g_ref[...]
    h = h * (1.0 + sc_ref[0]) + sh_ref[0]
    p = jnp.dot(h.astype(BF16), w_ref[...], preferred_element_type=F32)
    hy_ref[0] = p[:, :HY_COLS]

    lane = lax.broadcasted_iota(jnp.int32, (tm, LANES), 1)
    cg, sg, cd, sd = cg_ref[...], sg_ref[...], cd_ref[...], sd_ref[...]
    bd = bd_ref[...]

    def head_norm(xc, gain):
        ms = jnp.dot(xc * xc, bd, precision=HIGHEST, preferred_element_type=F32) * (1.0 / HEAD_DIM)
        return xc * lax.rsqrt(ms + NORM_EPS) * gain

    for c in range(GQA_Q_COLS // LANES):
        xc = p[:, Q_START + c * LANES: Q_START + (c + 1) * LANES]
        xc = head_norm(xc, qg_ref[...])
        q_ref[0, :, c * LANES:(c + 1) * LANES] = _rope(xc, cg, sg, HEAD_DIM // 4, lane).astype(BF16)
    dscale = DIFF_QK_DIM ** -0.5 * LOG2E
    for c in range(DIFF_Q_COLS // LANES):
        o = Q_START + GQA_Q_COLS + c * LANES
        xc = p[:, o:o + LANES] * dscale
        q_ref[0, :, GQA_Q_COLS + c * LANES: GQA_Q_COLS + (c + 1) * LANES] = (
            _rope(xc, cd, sd, DIFF_QK_DIM // 4, lane).astype(BF16))

    gk = head_norm(p[:, K_START:K_START + GQA_KV_COLS], kg_ref[...])
    k_ref[0, :, :GQA_KV_COLS] = _rope(gk, cg, sg, HEAD_DIM // 4, lane).astype(BF16)
    dk0 = K_START + 2 * GQA_KV_COLS
    for c in range(DIFF_Q_COLS // LANES):
        xc = p[:, dk0 + c * LANES: dk0 + (c + 1) * LANES]
        k_ref[0, :, GQA_KV_COLS + c * LANES: GQA_KV_COLS + (c + 1) * LANES] = (
            _rope(xc, cd, sd, DIFF_QK_DIM // 4, lane).astype(BF16))
    kb = vt_ref.shape[3]
    for sblk in range(tm // kb):
        rows = slice(sblk * kb, (sblk + 1) * kb)
        vals = jnp.concatenate([p[rows, K_START + GQA_KV_COLS:K_START + 2 * GQA_KV_COLS],
                                p[rows, dk0 + DIFF_Q_COLS:]], axis=1)
        vals_t = vals.T.astype(BF16)
        ones = jnp.ones((V_HEAD_ROWS - HEAD_DIM, kb), BF16)
        for hd in range(KV_COLS // HEAD_DIM):
            vt_ref[0, sblk, hd * V_HEAD_ROWS:hd * V_HEAD_ROWS + HEAD_DIM, :] = vals_t[hd * HEAD_DIM:(hd + 1) * HEAD_DIM]
            vt_ref[0, sblk, hd * V_HEAD_ROWS + HEAD_DIM:(hd + 1) * V_HEAD_ROWS, :] = ones


def _inproj(x, scale, shift, g, w, qg, kg, bd, tables, tm, kb):
    b, t, d = x.shape
    row = lambda bi, i: (bi, i, 0)
    per_b = lambda bi, i: (bi, 0, 0)
    const = lambda bi, i: (0, 0)
    tab = pl.BlockSpec((tm, LANES), lambda bi, i: (i, 0))
    return pl.pallas_call(
        _inproj_body,
        grid=(b, t // tm),
        in_specs=[pl.BlockSpec((1, tm, d), row),
                  pl.BlockSpec((1, 1, d), per_b), pl.BlockSpec((1, 1, d), per_b),
                  pl.BlockSpec((1, d), const),
                  pl.BlockSpec((d, IN_COLS), const),
                  pl.BlockSpec((1, LANES), const), pl.BlockSpec((1, LANES), const),
                  pl.BlockSpec((LANES, LANES), const),
                  tab, tab, tab, tab],
        out_specs=[pl.BlockSpec((1, tm, HY_COLS), row), pl.BlockSpec((1, tm, Q_COLS), row),
                   pl.BlockSpec((1, tm, KV_COLS), row),
                   pl.BlockSpec((1, tm // kb, VT_ROWS, kb), lambda bi, i: (bi, i, 0, 0))],
        out_shape=[jax.ShapeDtypeStruct((b, t, HY_COLS), F32), jax.ShapeDtypeStruct((b, t, Q_COLS), BF16),
                   jax.ShapeDtypeStruct((b, t, KV_COLS), BF16),
                   jax.ShapeDtypeStruct((b, t // kb, VT_ROWS, kb), BF16)],
        compiler_params=_cparams("parallel", "parallel"),
    )(x, scale, shift, g, w, qg, kg, bd, *tables)


def _filter_body(f_ref, w1_ref, b1_ref, w2_ref, b2_ref, w3_ref, fr_ref, dc_ref, o_ref):
    n = f_ref.shape[0]
    fr = fr_ref[...]
    h = jnp.sin(fr * (jnp.dot(f_ref[...], w1_ref[...], precision=HIGHEST, preferred_element_type=F32) + b1_ref[...]))
    h = jnp.sin(fr * (jnp.dot(h, w2_ref[...], precision=HIGHEST, preferred_element_type=F32) + b2_ref[...]))
    k = jnp.dot(h, w3_ref[...], precision=HIGHEST, preferred_element_type=F32)
    t = lax.broadcasted_iota(jnp.int32, k.shape, 0)
    off = jnp.abs(t - n // 2).astype(F32) * (2.0 / n)
    k = k * jnp.exp(-off * jnp.abs(dc_ref[...]))
    k = k / jnp.sum(jnp.abs(k), axis=0, keepdims=True)
    o_ref[...] = k.astype(BF16)


def _hyena_filters(feats, w1p, b1, w2, b2, w3, freq, decay):
    n = feats.shape[0]
    hid = w2.shape[0]
    cols = w3.shape[1]
    const = lambda j: (0, 0)
    return pl.pallas_call(
        _filter_body,
        grid=(cols // LANES,),
        in_specs=[pl.BlockSpec((n, LANES), const), pl.BlockSpec((LANES, hid), const),
                  pl.BlockSpec((1, hid), const), pl.BlockSpec((hid, hid), const), pl.BlockSpec((1, hid), const),
                  pl.BlockSpec((hid, LANES), lambda j: (0, j)), pl.BlockSpec((1, hid), const),
                  pl.BlockSpec((1, LANES), lambda j: (0, j))],
        out_specs=pl.BlockSpec((n, LANES), lambda j: (0, j)),
        out_shape=jax.ShapeDtypeStruct((n, cols), BF16),
        compiler_params=_cparams("parallel"),
    )(feats, w1p, b1, w2, b2, w3, freq, decay)


def _hyprep_body(p_ref, prev_ref, next_ref, w_ref, b_ref, v_ref, vb_ref, x1_ref, x2_ref, buf):
    i = pl.program_id(1)
    n = pl.num_programs(1)
    tm = p_ref.shape[1]
    buf[SUBLANES:tm + SUBLANES, :] = p_ref[0]
    buf[SUBLANES - 1:SUBLANES, :] = jnp.where(i > 0, prev_ref[0][SUBLANES - 1:SUBLANES, :], 0.0)
    buf[tm + SUBLANES:tm + SUBLANES + 1, :] = jnp.where(i < n - 1, next_ref[0][0:1, :], 0.0)
    w = w_ref[...]
    y = (buf[SUBLANES - 1:tm + SUBLANES - 1, :] * w[0:1] + buf[SUBLANES:tm + SUBLANES, :] * w[1:2]
         + buf[SUBLANES + 1:tm + SUBLANES + 1, :] * w[2:3] + b_ref[...])
    v_ref[...] = y[:, :HY_WIDTH]
    vb_ref[...] = y[:, :HY_WIDTH].astype(BF16)
    x1_ref[...] = y[:, HY_WIDTH:2 * HY_WIDTH]
    x2_ref[...] = y[:, 2 * HY_WIDTH:]


def _hyprep(p_hy, conv_w, conv_b, tm):
    b, t, _ = p_hy.shape
    nb8 = t // SUBLANES
    r8 = tm // SUBLANES
    tmaj = pl.BlockSpec((tm, HY_WIDTH), lambda bi, i: (i, bi))
    shp = lambda dt: jax.ShapeDtypeStruct((t, b * HY_WIDTH), dt)
    return pl.pallas_call(
        _hyprep_body,
        grid=(b, t // tm),
        in_specs=[pl.BlockSpec((1, tm, HY_COLS), lambda bi, i: (bi, i, 0)),
                  pl.BlockSpec((1, SUBLANES, HY_COLS), lambda bi, i: (bi, jnp.maximum(i * r8 - 1, 0), 0)),
                  pl.BlockSpec((1, SUBLANES, HY_COLS), lambda bi, i: (bi, jnp.minimum((i + 1) * r8, nb8 - 1), 0)),
                  pl.BlockSpec((3, HY_COLS), lambda bi, i: (0, 0)),
                  pl.BlockSpec((1, HY_COLS), lambda bi, i: (0, 0))],
        out_specs=[tmaj, tmaj, tmaj, tmaj],
        out_shape=[shp(F32), shp(BF16), shp(F32), shp(F32)],
        scratch_shapes=[pltpu.VMEM((tm + 2 * SUBLANES, HY_COLS), F32)],
        compiler_params=_cparams("parallel", "parallel"),
    )(p_hy, p_hy, p_hy, conv_w, conv_b)


def _dft_mats(n_t):
    n_fft = 3 * n_t // 2
    half = n_fft // 2
    step = 2.0 * math.pi / n_fft

    def tables(rows, first, width):
        r = jnp.arange(rows, dtype=jnp.int32)
        c = jnp.arange(width, dtype=jnp.int32)
        base = ((r[:, None] * c[None, :]) % n_fft).astype(F32) * step
        start = ((first[:, None] * c[None, :]) % n_fft).astype(F32) * step
        return jnp.cos(base), jnp.sin(base), jnp.cos(start)[:, None, :], jnp.sin(start)[:, None, :]

    tm = _tile_rows(half, 256)
    nblk = half // tm
    cb, sb, c0, s0 = tables(tm, jnp.arange(nblk, dtype=jnp.int32) * tm, n_t)
    tile = pl.BlockSpec((tm, n_t), lambda p, i: (0, 0))
    vec = pl.BlockSpec((1, 1, n_t), lambda p, i: (i, 0, 0))
    fwd = pl.pallas_call(
        _dftgen_fwd_body,
        grid=(2, nblk),
        in_specs=[tile, tile, vec, vec],
        out_specs=pl.BlockSpec((tm, n_t), lambda p, i: (p * nblk + i, 0)),
        out_shape=jax.ShapeDtypeStruct((n_fft, n_t), BF16),
        compiler_params=_cparams("parallel", "parallel"),
    )(cb, sb, c0, s0)

    tm = _tile_rows(n_t, 256)
    nblk = n_t // tm
    cb, sb, c0, s0 = tables(tm, jnp.arange(nblk, dtype=jnp.int32) * tm + n_t // 2, half)
    tile = pl.BlockSpec((tm, half), lambda i: (0, 0))
    vec = pl.BlockSpec((1, 1, half), lambda i: (i, 0, 0))
    out = pl.BlockSpec((tm, half), lambda i: (i, 0))
    g_re, g_im = pl.pallas_call(
        functools.partial(_dftgen_inv_body, first_sample=n_t // 2),
        grid=(nblk,),
        in_specs=[tile, tile, vec, vec],
        out_specs=[out, out],
        out_shape=[jax.ShapeDtypeStruct((n_t, half), BF16)] * 2,
        compiler_params=_cparams("parallel"),
    )(cb, sb, c0, s0)
    return fwd, g_re, g_im


def _dftgen_fwd_body(cb_ref, sb_ref, c0_ref, s0_ref, o_ref):
    cb, sb, c0, s0 = cb_ref[...], sb_ref[...], c0_ref[0], s0_ref[0]
    is_im = pl.program_id(0) == 1
    re = cb * c0 - sb * s0
    im = -(sb * c0 + cb * s0)
    col = lax.broadcasted_iota(jnp.int32, cb.shape, 1)
    row0 = (lax.broadcasted_iota(jnp.int32, cb.shape, 0) == 0) & (pl.program_id(1) == 0)
    im = jnp.where(row0, (1 - 2 * (col % 2)).astype(F32), im)
    o_ref[...] = jnp.where(is_im, im, re).astype(BF16)


def _dftgen_inv_body(cb_ref, sb_ref, c0_ref, s0_ref, re_ref, im_ref, *, first_sample):
    cb, sb, c0, s0 = cb_ref[...], sb_ref[...], c0_ref[0], s0_ref[0]
    re = 2.0 * (cb * c0 - sb * s0)
    im = -2.0 * (sb * c0 + cb * s0)
    col0 = lax.broadcasted_iota(jnp.int32, cb.shape, 1) == 0
    n = lax.broadcasted_iota(jnp.int32, cb.shape, 0) + pl.program_id(0) * cb.shape[0] + first_sample
    re_ref[...] = jnp.where(col0, 1.0, re).astype(BF16)
    im_ref[...] = jnp.where(col0, (1 - 2 * (n % 2)).astype(F32), im).astype(BF16)


def _dft_fwd_body(fre_ref, fim_ref, u_ref, zre_ref, zim_ref):
    u = u_ref[...]
    zre_ref[...] = jnp.dot(fre_ref[...], u, preferred_element_type=F32)
    zim_ref[...] = jnp.dot(fim_ref[...], u, preferred_element_type=F32)


def _dft_fwd_mul_body(fre_ref, fim_ref, u_ref, kre_ref, kim_ref, zre_ref, zim_ref):
    u = u_ref[...]
    xr = jnp.dot(fre_ref[...], u, preferred_element_type=F32)
    xi = jnp.dot(fim_ref[...], u, preferred_element_type=F32)
    kre, kim = kre_ref[...], kim_ref[...]
    row0 = (lax.broadcasted_iota(jnp.int32, xr.shape, 0) == 0) & (pl.program_id(0) == 0)
    a = xr * kre
    bb = xi * kim
    zre_ref[...] = (a - jnp.where(row0, 0.0, bb)).astype(zre_ref.dtype)
    zim_ref[...] = jnp.where(row0, bb, xr * kim + xi * kre).astype(zim_ref.dtype)


def _dft_fwd(fwd, u, tm, kf=None, order=0):
    n_fft, n_t = fwd.shape
    half = n_fft // 2
    cols = u.shape[1]
    nblk = half // tm
    tn = HY_WIDTH
    in_specs = [pl.BlockSpec((tm, n_t), lambda i, j: (i, 0)),
                pl.BlockSpec((tm, n_t), lambda i, j: (i + nblk, 0)),
                pl.BlockSpec((n_t, tn), lambda i, j: (0, j))]
    args = [fwd, fwd, u]
    if kf is None:
        body, odt = _dft_fwd_body, F32
    else:
        body, odt = _dft_fwd_mul_body, BF16
        in_specs += [pl.BlockSpec((tm, tn), lambda i, j: (i, order)), pl.BlockSpec((tm, tn), lambda i, j: (i, order))]
        args += [kf[0], kf[1]]
    out = pl.BlockSpec((tm, tn), lambda i, j: (i, j))
    return pl.pallas_call(
        body,
        grid=(nblk, cols // tn),
        in_specs=in_specs,
        out_specs=[out, out],
        out_shape=[jax.ShapeDtypeStruct((half, cols), odt)] * 2,
        compiler_params=_cparams("parallel", "parallel"),
    )(*args)


def _dft_inv_body(gre_ref, gim_ref, zre_ref, zim_ref, u_ref, x_ref, bias_ref, *out_refs, inv_n):
    y = (jnp.dot(gre_ref[...], zre_ref[...], preferred_element_type=F32)
         + jnp.dot(gim_ref[...], zim_ref[...], preferred_element_type=F32)) * inv_n
    o = x_ref[...] * (y + u_ref[...] * bias_ref[...])
    if len(out_refs) == 2:
        out_refs[0][...] = o
        out_refs[1][...] = o.astype(BF16)
    else:
        out_refs[0][0] = o.astype(BF16)


def _dft_inv(g_re, g_im, zre, zim, u, xg, bias, tm, batch, last):
    n_t, half = g_re.shape
    cols = u.shape[1]
    tn = HY_WIDTH
    gsp = pl.BlockSpec((tm, half), lambda i, j: (i, 0))
    zsp = pl.BlockSpec((half, tn), lambda i, j: (0, j))
    tsp = pl.BlockSpec((tm, tn), lambda i, j: (i, j))
    if last:
        out_specs = [pl.BlockSpec((1, tm, tn), lambda i, j: (j, i, 0))]
        out_shape = [jax.ShapeDtypeStruct((batch, n_t, tn), BF16)]
    else:
        out_specs = [tsp, tsp]
        out_shape = [jax.ShapeDtypeStruct((n_t, cols), F32), jax.ShapeDtypeStruct((n_t, cols), BF16)]
    return pl.pallas_call(
        functools.partial(_dft_inv_body, inv_n=1.0 / (2 * half)),
        grid=(n_t // tm, cols // tn),
        in_specs=[gsp, gsp, zsp, zsp, tsp, tsp, pl.BlockSpec((1, tn), lambda i, j: (0, 0))],
        out_specs=out_specs,
        out_shape=out_shape,
        compiler_params=_cparams("parallel", "parallel"),
    )(g_re, g_im, zre, zim, u, xg, bias)


def _hyena(p_hy, conv_w, conv_b, hy_bias, kf, mats, tm_prep, tm_f, tm_t):
    b = p_hy.shape[0]
    fwd, g_re, g_im = mats
    v, vb, x1, x2 = _hyprep(p_hy, conv_w, conv_b, tm_prep)
    zre, zim = _dft_fwd(fwd, vb, tm_f, kf, 0)
    y1, y1b = _dft_inv(g_re, g_im, zre, zim, v, x1, hy_bias[0:1], tm_t, b, False)
    zre, zim = _dft_fwd(fwd, y1b, tm_f, kf, 1)
    (hy,) = _dft_inv(g_re, g_im, zre, zim, y1, x2, hy_bias[1:2], tm_t, b, True)
    return hy


def _attn_body(q_ref, k_ref, vt_ref, lam_ref, sub_ref, go_ref, do_ref, *, lam_init, group, ahead):
    tq = q_ref.shape[1]
    nkb, _, kb = vt_ref.shape[1:]
    lp = lam_ref[...]
    lam = (jnp.exp(jnp.sum(lp[0:1] * lp[1:2], axis=1, keepdims=True))
           - jnp.exp(jnp.sum(lp[2:3] * lp[3:4], axis=1, keepdims=True)) + lam_init)

    def attend(maps):
        def score(i, qm, kcol, kw):
            off = pl.multiple_of(i * kb, kb)
            return lax.dot_general(k_ref[0, pl.ds(off, kb), kcol:kcol + kw], qm, (((1,), (1,)), ((), ())),
                                   preferred_element_type=F32)

        def update(i, s, m, acc, vrow):
            vblk = vt_ref[0, i, vrow:vrow + V_HEAD_ROWS, :]
            m_new = jnp.maximum(m, jnp.max(s, axis=0, keepdims=True))
            alpha = jnp.exp2(m - m_new)
            p = jnp.exp2((s - m_new).astype(BF16))
            return m_new, alpha * acc + jnp.dot(vblk, p, preferred_element_type=F32)

        def body(i, state):
            scores, new = {}, []
            for j in range(len(maps) + ahead):
                if j < len(maps):
                    scores[j] = score(i, *maps[j][:3])
                if j >= ahead:
                    jj = j - ahead
                    new.append(update(i, scores.pop(jj), *state[jj], maps[jj][3]))
            return tuple(new)

        init = tuple((jnp.full((1, tq), NEG_BIG, F32), jnp.zeros((V_HEAD_ROWS, tq), F32)) for _ in maps)
        return [acc[:HEAD_DIM] / acc[HEAD_DIM:HEAD_DIM + 1] for (_, acc) in lax.fori_loop(0, nkb, body, init)]

    lane = lax.broadcasted_iota(jnp.int32, (tq, LANES), 1)
    lane_d = lax.broadcasted_iota(jnp.int32, (tq, DIFF_Q_COLS), 1)
    dq = q_ref[0, :, GQA_Q_COLS:]
    maps = []
    for c in range(GQA_Q_COLS // LANES):
        qc = q_ref[0, :, c * LANES:(c + 1) * LANES]
        for j in range(GQA_KV_HEADS):
            maps.append((jnp.where((lane // HEAD_DIM) == j, qc, jnp.zeros_like(qc)), 0, GQA_KV_COLS, j * V_HEAD_ROWS))
    for h in range(DIFF_HEADS):
        for cmap in range(2):
            qm = jnp.where((lane_d // DIFF_QK_DIM) == 2 * h + cmap, dq, jnp.zeros_like(dq))
            maps.append((qm, GQA_KV_COLS, DIFF_Q_COLS, (GQA_KV_HEADS + h) * V_HEAD_ROWS))
    outs = []
    for g in range(0, len(maps), group):
        outs += attend(maps[g:g + group])

    n_gqa = GQA_HEADS
    for c in range(GQA_Q_COLS // LANES):
        pair = jnp.concatenate(outs[GQA_KV_HEADS * c:GQA_KV_HEADS * (c + 1)], axis=0)
        go_ref[0, :, c * LANES:(c + 1) * LANES] = pair.T.astype(BF16)
    sub = sub_ref[...] * (1.0 - lam_init)
    heads = []
    for h in range(DIFF_HEADS):
        d = outs[n_gqa + 2 * h] - lam * outs[n_gqa + 2 * h + 1]
        ms = jnp.mean(d * d, axis=0, keepdims=True)
        heads.append(d * lax.rsqrt(ms + NORM_EPS) * sub)
    do_ref[0] = jnp.concatenate(heads, axis=0).T.astype(BF16)


def _attention(q, k, vt, lam_p, sub, lam_init, tq, group=16, ahead=10):
    b, t, _ = q.shape
    lk = k.shape[1]
    row = lambda bi, i: (bi, i, 0)
    sub_t = jnp.broadcast_to(sub.reshape(DIFF_V_DIM, 1), (DIFF_V_DIM, tq))
    return pl.pallas_call(
        functools.partial(_attn_body, lam_init=lam_init, group=group, ahead=ahead),
        grid=(b, t // tq),
        in_specs=[pl.BlockSpec((1, tq, Q_COLS), row),
                  pl.BlockSpec((1, lk, KV_COLS), lambda bi, i: (bi, 0, 0)),
                  pl.BlockSpec((1,) + vt.shape[1:], lambda bi, i: (bi, 0, 0, 0)),
                  pl.BlockSpec(lam_p.shape, lambda bi, i: (0, 0)),
                  pl.BlockSpec((DIFF_V_DIM, tq), lambda bi, i: (0, 0))],
        out_specs=[pl.BlockSpec((1, tq, GQA_Q_COLS), row), pl.BlockSpec((1, tq, DIFF_V_COLS), row)],
        out_shape=[jax.ShapeDtypeStruct((b, t, GQA_Q_COLS), BF16), jax.ShapeDtypeStruct((b, t, DIFF_V_COLS), BF16)],
        compiler_params=_cparams("parallel", "parallel"),
    )(q, k, vt, lam_p, sub_t)


def _outproj_body(hy_ref, go_ref, do_ref, w_ref, x_ref, g1_ref, gpost_ref, gpre_ref, sc_ref, sh_ref,
                  wr_ref, br_ref, xm_ref, h2_ref, topi_ref, topw_ref, cnt_ref):
    y = (jnp.dot(hy_ref[0], w_ref[:HY_WIDTH], preferred_element_type=F32)
         + jnp.dot(go_ref[0], w_ref[HY_WIDTH:HY_WIDTH + GQA_Q_COLS], preferred_element_type=F32)
         + jnp.dot(do_ref[0], w_ref[HY_WIDTH + GQA_Q_COLS:], preferred_element_type=F32))
    xm = x_ref[0] + g1_ref[0] * (_rms(y) * gpost_ref[...])
    xm_ref[0] = xm
    h2 = _rms(xm) * gpre_ref[...] * (1.0 + sc_ref[0]) + sh_ref[0]
    h2_ref[0] = h2

    h_hi = h2.astype(BF16)
    h_lo = (h2 - h_hi.astype(F32)).astype(BF16)
    logits = (jnp.dot(h_hi, wr_ref[0], preferred_element_type=F32) + jnp.dot(h_lo, wr_ref[0], preferred_element_type=F32)
              + jnp.dot(h_hi, wr_ref[1], preferred_element_type=F32) + br_ref[...])
    lane = lax.broadcasted_iota(jnp.int32, logits.shape, 1)
    topi = jnp.zeros(logits.shape, jnp.int32)
    topw = jnp.zeros_like(logits)
    chosen = jnp.zeros_like(logits)
    denom = jnp.zeros((logits.shape[0], 1), F32)
    top = None
    for k in range(TOP_K):
        mx = jnp.max(logits, axis=1, keepdims=True)
        first = jnp.min(jnp.where(logits == mx, lane, LANES), axis=1, keepdims=True)
        sel = lane == first
        top = mx if top is None else top
        e = jnp.exp(mx - top)
        topi = jnp.where(lane == k, first, topi)
        topw = jnp.where(lane == k, e, topw)
        chosen = jnp.where(sel, 1.0, chosen)
        denom = denom + e
        logits = jnp.where(sel, NEG_BIG, logits)
    topi_ref[0] = topi[:, :TOP_K]
    topw_ref[0] = (topw / denom)[:, :TOP_K]

    @pl.when((pl.program_id(0) == 0) & (pl.program_id(1) == 0))
    def _():
        cnt_ref[...] = jnp.zeros_like(cnt_ref)
    cnt_ref[...] += jnp.sum(chosen, axis=0, keepdims=True)


def _outproj(hy, go, do, w_out, x, g1, gpost, gpre, sc2, sh2, wr, br, tm):
    b, t, d = x.shape
    row = lambda bi, i: (bi, i, 0)
    per_b = lambda bi, i: (bi, 0, 0)
    const = lambda bi, i: (0, 0)
    vec = pl.BlockSpec((1, 1, d), per_b)
    return pl.pallas_call(
        _outproj_body,
        grid=(b, t // tm),
        in_specs=[pl.BlockSpec((1, tm, HY_WIDTH), row), pl.BlockSpec((1, tm, GQA_Q_COLS), row),
                  pl.BlockSpec((1, tm, DIFF_V_COLS), row), pl.BlockSpec(w_out.shape, const),
                  pl.BlockSpec((1, tm, d), row), vec, pl.BlockSpec((1, d), const), pl.BlockSpec((1, d), const),
                  vec, vec, pl.BlockSpec((2, d, LANES), lambda bi, i: (0, 0, 0)), pl.BlockSpec((1, LANES), const)],
        out_specs=[pl.BlockSpec((1, tm, d), row), pl.BlockSpec((1, tm, d), row),
                   pl.BlockSpec((1, tm, TOP_K), row), pl.BlockSpec((1, tm, TOP_K), row),
                   pl.BlockSpec((SUBLANES, LANES), const)],
        out_shape=[jax.ShapeDtypeStruct((b, t, d), F32), jax.ShapeDtypeStruct((b, t, d), F32),
                   jax.ShapeDtypeStruct((b, t, TOP_K), jnp.int32), jax.ShapeDtypeStruct((b, t, TOP_K), F32),
                   jax.ShapeDtypeStruct((SUBLANES, LANES), F32)],
        compiler_params=_cparams("arbitrary", "arbitrary"),
    )(hy, go, do, w_out, x, g1, gpost, gpre, sc2, sh2, wr, br)


EXPERT_TILE = 256


def _positions_body(topi_ref, base_ref, pos_ref, carry_ref):
    tm = topi_ref.shape[0]

    @pl.when(pl.program_id(0) == 0)
    def _():
        carry_ref[...] = jnp.zeros_like(carry_ref)

    topi = topi_ref[...]
    lane = lax.broadcasted_iota(jnp.int32, (tm, LANES), 1)
    hits = [lane == topi[:, k:k + 1] for k in range(TOP_K)]
    chosen = jnp.zeros((tm, LANES), F32)
    for hit in hits:
        chosen = jnp.where(hit, 1.0, chosen)
    r = lax.broadcasted_iota(jnp.int32, (tm, tm), 0)
    c = lax.broadcasted_iota(jnp.int32, (tm, tm), 1)
    before = jnp.where(c < r, 1.0, 0.0).astype(BF16)
    rank = jnp.dot(before, chosen.astype(BF16), preferred_element_type=F32) + carry_ref[0:1, :] + base_ref[...]
    pos = jnp.zeros((tm, LANES), F32)
    for k, hit in enumerate(hits):
        pos = jnp.where(lane == k, jnp.sum(jnp.where(hit, rank, 0.0), axis=1, keepdims=True), pos)
    pos_ref[...] = pos[:, :TOP_K].astype(jnp.int32)
    carry_ref[...] += jnp.sum(chosen, axis=0, keepdims=True)


def _positions(topi, base, tm):
    n = topi.shape[0]
    return pl.pallas_call(
        _positions_body,
        grid=(n // tm,),
        in_specs=[pl.BlockSpec((tm, TOP_K), lambda i: (i, 0)), pl.BlockSpec((1, LANES), lambda i: (0, 0))],
        out_specs=pl.BlockSpec((tm, TOP_K), lambda i: (i, 0)),
        out_shape=jax.ShapeDtypeStruct((n, TOP_K), jnp.int32),
        scratch_shapes=[pltpu.VMEM((SUBLANES, LANES), F32)],
        compiler_params=_cparams("arbitrary"),
    )(topi, base)


def _row_copy(src, src_row, dst, dst_row, sem):
    return pltpu.make_async_copy(src.at[pl.ds(src_row, 1)], dst.at[pl.ds(dst_row, 1)], sem)


def _scatter_body(seg_ref, pos_ref, h_ref, xs_ref, zero_ref, sem, zsem):
    i = pl.program_id(0)
    n_groups = h_ref.shape[0]
    n_exp = seg_ref.shape[1]

    @pl.when(i == 0)
    def _():
        zero_ref[...] = jnp.zeros_like(zero_ref)
        for wait in (False, True):
            def per_expert(e, _):
                def per_row(r, _):
                    cp = _row_copy(zero_ref, 0, xs_ref, seg_ref[0, e] + r, zsem)
                    cp.wait() if wait else cp.start()
                    return 0
                return lax.fori_loop(seg_ref[1, e], seg_ref[2, e], per_row, 0)
            lax.fori_loop(0, n_exp, per_expert, 0)

    for wait in (False, True):
        def per_group(g, _):
            p0 = g * (SUBLANES * TOP_K)
            for j in range(SUBLANES):
                for k in range(TOP_K):
                    cp = pltpu.make_async_copy(h_ref.at[g, pl.ds(j, 1)],
                                               xs_ref.at[pl.ds(pos_ref[p0 + j * TOP_K + k], 1)], sem)
                    cp.wait() if wait else cp.start(priority=k % 2)
            return 0
        lax.fori_loop(0, n_groups, per_group, 0)


def _scatter(seg, pos_flat, h2, n_rows, ts):
    n, d = h2.shape
    h2 = h2.reshape(n // SUBLANES, SUBLANES, d)
    return pl.pallas_call(
        _scatter_body,
        grid_spec=pltpu.PrefetchScalarGridSpec(
            num_scalar_prefetch=1,
            grid=(n // ts,),
            in_specs=[pl.BlockSpec((ts * TOP_K,), lambda i, seg: (i,), memory_space=pltpu.SMEM),
                      pl.BlockSpec((ts // SUBLANES, SUBLANES, d), lambda i, seg: (i, 0, 0))],
            out_specs=pl.BlockSpec(memory_space=pl.ANY),
            scratch_shapes=[pltpu.VMEM((SUBLANES, d), F32), pltpu.SemaphoreType.DMA(()), pltpu.SemaphoreType.DMA(())],
        ),
        out_shape=jax.ShapeDtypeStruct((n_rows, d), F32),
        compiler_params=_cparams("arbitrary"),
    )(seg, pos_flat, h2)


def _expert_body(te_ref, na_ref, x_ref, wgu_ref, bgu_ref, wd_ref, bd_ref, y_ref, wgu_b, wd_b):
    i = pl.program_id(0)
    ff = wd_ref.shape[1]
    active = i < na_ref[0]
    new_expert = (i == 0) | (te_ref[i] != te_ref[jnp.maximum(i - 1, 0)])

    @pl.when(active & new_expert)
    def _():
        wgu_b[...] = wgu_ref[0].astype(BF16)
        wd_b[...] = wd_ref[0].astype(BF16)

    @pl.when(active)
    def _():
        gu = jnp.dot(x_ref[...].astype(BF16), wgu_b[...], preferred_element_type=F32) + bgu_ref[0]
        gate = jnp.minimum(gu[:, :ff], SWIGLU_LIMIT)
        up = jnp.clip(gu[:, ff:], -SWIGLU_LIMIT, SWIGLU_LIMIT)
        act = (up + 1.0) * gate / (1.0 + jnp.exp(-SWIGLU_ALPHA * gate))
        y_ref[...] = jnp.dot(act.astype(BF16), wd_b[...], preferred_element_type=F32) + bd_ref[0]


def _experts(tile_expert, n_active, xs, wgu, bgu, wd, bd, layer):
    n_rows, d = xs.shape
    depth, n_exp, _, ff2 = wgu.shape
    ff = ff2 // 2
    tile = lambda i, te, na: (jnp.minimum(i, na[0] - 1), 0)
    first_expert = layer * n_exp
    wsel = lambda i, te, na: (first_expert + te[jnp.minimum(i, na[0] - 1)], 0, 0)
    n_exp = depth * n_exp
    wgu = wgu.reshape(n_exp, d, ff2)
    wd = wd.reshape(n_exp, ff, d)
    return pl.pallas_call(
        _expert_body,
        grid_spec=pltpu.PrefetchScalarGridSpec(
            num_scalar_prefetch=2,
            grid=(n_rows // EXPERT_TILE,),
            in_specs=[pl.BlockSpec((EXPERT_TILE, d), tile),
                      pl.BlockSpec((1, d, ff2), wsel), pl.BlockSpec((1, 1, ff2), wsel),
                      pl.BlockSpec((1, ff, d), wsel), pl.BlockSpec((1, 1, d), wsel)],
            out_specs=pl.BlockSpec((EXPERT_TILE, d), tile),
            scratch_shapes=[pltpu.VMEM((d, ff2), BF16), pltpu.VMEM((ff, d), BF16)],
        ),
        out_shape=jax.ShapeDtypeStruct((n_rows, d), F32),
        compiler_params=_cparams("arbitrary"),
    )(tile_expert, n_active, xs, wgu, bgu.reshape(n_exp, 1, ff2), wd, bd.reshape(n_exp, 1, d))


def _combine_body(pos_ref, w_ref, x_ref, g2_ref, gpost_ref, ys_ref, o_ref, buf, sem):
    tc, d = x_ref.shape
    for wait in (False, True):
        def per_group(g, _):
            p0 = g * (SUBLANES * TOP_K)
            for j in range(SUBLANES):
                for k in range(TOP_K):
                    cp = pltpu.make_async_copy(ys_ref.at[pl.ds(pos_ref[p0 + j * TOP_K + k], 1)],
                                               buf.at[k, g, pl.ds(j, 1)], sem)
                    cp.wait() if wait else cp.start(priority=k % 2)
            return 0
        lax.fori_loop(0, tc // SUBLANES, per_group, 0)
    w = w_ref[...]
    y = w[:, 0:1] * buf[0].reshape(tc, d)
    for k in range(1, TOP_K):
        y = y + w[:, k:k + 1] * buf[k].reshape(tc, d)
    o_ref[...] = x_ref[...] + g2_ref[0] * (_rms(y) * gpost_ref[...])


def _combine(pos_flat, topw, xm, g2, gpost, ys, tokens_per_batch, tc):
    n, d = xm.shape
    per_tile = tokens_per_batch // tc
    tok = lambda i: (i, 0)
    return pl.pallas_call(
        _combine_body,
        grid=(n // tc,),
        in_specs=[pl.BlockSpec((tc * TOP_K,), lambda i: (i,), memory_space=pltpu.SMEM),
                  pl.BlockSpec((tc, TOP_K), tok), pl.BlockSpec((tc, d), tok),
                  pl.BlockSpec((1, 1, d), lambda i: (i // per_tile, 0, 0)), pl.BlockSpec((1, d), lambda i: (0, 0)),
                  pl.BlockSpec(memory_space=pl.ANY)],
        out_specs=pl.BlockSpec((tc, d), tok),
        out_shape=jax.ShapeDtypeStruct((n, d), F32),
        scratch_shapes=[pltpu.VMEM((TOP_K, tc // SUBLANES, SUBLANES, d), F32), pltpu.SemaphoreType.DMA(())],
        compiler_params=_cparams("arbitrary"),
    )(pos_flat, topw, xm, g2, gpost, ys)


def _moe(h2, topi, topw, counts, wgu, bgu, wd, bd, layer, xm, g2, gpost, tm):
    b, t, d = xm.shape
    n_exp = wgu.shape[1]
    n = b * t
    cnt = counts[0, :n_exp].astype(jnp.int32)
    padded = (cnt + EXPERT_TILE - 1) // EXPERT_TILE * EXPERT_TILE
    ends = jnp.cumsum(padded)
    first = ends - padded
    n_rows = n * TOP_K + n_exp * EXPERT_TILE
    tile_start = jnp.arange(n_rows // EXPERT_TILE, dtype=jnp.int32) * EXPERT_TILE
    tile_expert = jnp.minimum(jnp.sum(tile_start[:, None] >= ends[None, :], axis=1), n_exp - 1).astype(jnp.int32)
    n_active = (ends[-1:] // EXPERT_TILE).astype(jnp.int32)
    base = jnp.zeros((1, LANES), F32).at[0, :n_exp].set(first.astype(F32))
    seg = jnp.stack([first, cnt, padded]).astype(jnp.int32)

    pos = _positions(topi.reshape(n, TOP_K), base, tm)
    pos_flat = pos.reshape(n * TOP_K)
    xs = _scatter(seg, pos_flat, h2.reshape(n, d), n_rows, tm)
    ys = _experts(tile_expert, n_active, xs, wgu, bgu, wd, bd, layer)
    out = _combine(pos_flat, topw.reshape(n, TOP_K), xm.reshape(n, d), g2, gpost, ys, t, min(tm, 256))
    return out.reshape(b, t, d)


def _rope_tables(n_tok, dim, use_pos):
    if not use_pos:
        return jnp.ones((n_tok, LANES), F32), jnp.zeros((n_tok, LANES), F32)
    half = dim // 4
    t = jnp.arange(n_tok, dtype=jnp.int32)
    row = (t // GRID_W).astype(F32)
    col = (t % GRID_W).astype(F32)
    inv_freq = ROPE_THETA ** (-jnp.arange(half, dtype=F32) / half)
    a_row = row[:, None] * inv_freq[None, :]
    a_col = col[:, None] * inv_freq[None, :]
    cos = jnp.concatenate([jnp.cos(a_row)] * 2 + [jnp.cos(a_col)] * 2, axis=1)
    sin = jnp.concatenate([-jnp.sin(a_row), jnp.sin(a_row), -jnp.sin(a_col), jnp.sin(a_col)], axis=1)
    reps = LANES // dim
    return jnp.tile(cos, (1, reps)), jnp.tile(sin, (1, reps))


def _hyena_feats(n_t):
    t = jnp.arange(n_t, dtype=F32)
    bands = jnp.linspace(1e-4, HY_BANDS - 1, HY_BANDS, dtype=F32)
    w = (2.0 * math.pi / n_t) * t
    feats = jnp.concatenate([(t / n_t)[:, None], jnp.cos(w[:, None] * bands), -jnp.sin(w[:, None] * bands)], axis=-1)
    return jnp.pad(feats, ((0, 0), (0, LANES - feats.shape[1])))


def _tile_rows(t, cap):
    return cap if t % cap == 0 else t


def kernel(x, c, ctx, c_ctx, w_ada, b_ada, g_pre_mix, g_post_mix, g_pre_ffn, g_post_ffn, w_in, w_out, hy_conv_w,
           hy_conv_b, hy_w1, hy_b1, hy_w2, hy_b2, hy_w3, hy_freq, hy_decay, hy_bias, q_norm, k_norm, lam_q1, lam_k1,
           lam_q2, lam_k2, subln, w_router, b_router, w_gu, b_gu, w_down, b_down):
    b, s, d = x.shape
    n_ctx = ctx.shape[1]
    depth = w_ada.shape[0]
    n_exp = w_router.shape[-1]

    rows = -(-(b + 1) // SUBLANES) * SUBLANES
    cc = jnp.zeros((rows, d), F32).at[:b].set(c).at[b].set(c_ctx)
    mod = _ada(cc, w_ada, b_ada)

    head_cols = jnp.concatenate([jnp.arange(HEAD_DIM) + HEAD_DIM * h for h in _GQA_HEAD_ORDER])
    in_perm = jnp.concatenate([jnp.arange(Q_START), Q_START + head_cols, jnp.arange(Q_START + GQA_Q_COLS, IN_COLS)])
    out_perm = jnp.concatenate([jnp.arange(HY_WIDTH), HY_WIDTH + head_cols, jnp.arange(HY_WIDTH + GQA_Q_COLS, d)])
    w_in_b = w_in[:, :, in_perm].astype(BF16)
    w_out_b = w_out[:, out_perm, :].astype(BF16)
    wr_p = jnp.pad(w_router, ((0, 0), (0, 0), (0, LANES - n_exp)))
    wr_hi = wr_p.astype(BF16)
    wr_p = jnp.stack([wr_hi, (wr_p - wr_hi.astype(F32)).astype(BF16)], axis=1)
    br_p = jnp.pad(b_router, ((0, 0), (0, LANES - n_exp)), constant_values=NEG_BIG)[:, None, :]

    lane = jnp.arange(LANES)
    block_diag = (lane[:, None] // HEAD_DIM == lane[None, :] // HEAD_DIM).astype(F32)
    reps = LANES // HEAD_DIM

    tabs_lat = _rope_tables(s, HEAD_DIM, True) + _rope_tables(s, DIFF_QK_DIM, True)
    tabs_ctx = _rope_tables(n_ctx, HEAD_DIM, False) + _rope_tables(n_ctx, DIFF_QK_DIM, False)
    mats_lat = _dft_mats(s)
    feats_lat = _hyena_feats(s)
    if depth > 1:
        mats_ctx = _dft_mats(n_ctx)
        feats_ctx = _hyena_feats(n_ctx)

    tm_lat = _tile_rows(s, 512)
    tm_ctx = _tile_rows(n_ctx, 256)
    lk = n_ctx + s
    kb = 256 if (tm_lat % 256 == 0 and tm_ctx % 256 == 0) else 128

    def filters(i, feats, mats):
        w1p = jnp.pad(hy_w1[i], ((0, LANES - hy_w1.shape[1]), (0, 0)))
        k = _hyena_filters(feats, w1p, hy_b1[i][None], hy_w2[i], hy_b2[i][None], hy_w3[i], hy_freq[i][None],
                           hy_decay[i][None])
        half = mats[0].shape[0] // 2
        return _dft_fwd(mats[0], k, _tile_rows(half, 512))

    xc = ctx
    for i in range(depth):
        ctx_out = i < depth - 1
        lam_init = 0.8 - 0.6 * math.exp(-0.3 * i)
        m_lat = mod[i, :b].reshape(b, 1, 6 * d)
        sh1, sc1, g1, sh2, sc2, g2 = [m_lat[:, :, j * d:(j + 1) * d] for j in range(6)]
        m_ctx = jnp.broadcast_to(mod[i, b].reshape(1, 1, 6 * d), (b, 1, 6 * d))
        csh1, csc1, cg1, csh2, csc2, cg2 = [m_ctx[:, :, j * d:(j + 1) * d] for j in range(6)]

        qg = jnp.tile(q_norm[i], reps)[None] * (HEAD_DIM ** -0.5 * LOG2E)
        kg = jnp.tile(k_norm[i], reps)[None]
        lam_p = jnp.stack([lam_q1[i], lam_k1[i], lam_q2[i], lam_k2[i]])
        proj = functools.partial(_inproj, g=g_pre_mix[i][None], w=w_in_b[i], qg=qg, kg=kg, bd=block_diag, kb=kb)

        hy_c, q_c, k_c, v_c = proj(xc, csc1, csh1, tables=tabs_ctx, tm=tm_ctx)
        hy_l, q_l, k_l, v_l = proj(x, sc1, sh1, tables=tabs_lat, tm=tm_lat)
        k_all = jnp.concatenate([k_c, k_l], axis=1)
        v_all = jnp.concatenate([v_c, v_l], axis=1)

        attn = functools.partial(_attention, lam_p=lam_p, sub=subln[i], lam_init=lam_init)
        go_l, do_l = attn(q_l, k_all, v_all, tq=_tile_rows(s, 256))
        kf_lat = filters(i, feats_lat, mats_lat)
        half_lat = mats_lat[0].shape[0] // 2
        hyo_l = _hyena(hy_l, hy_conv_w[i], hy_conv_b[i][None], hy_bias[i], kf_lat, mats_lat,
                       tm_lat, _tile_rows(half_lat, 512), tm_lat)

        post = functools.partial(_outproj, w_out=w_out_b[i], gpost=g_post_mix[i][None], gpre=g_pre_ffn[i][None],
                                 wr=wr_p[i], br=br_p[i])
        moe = functools.partial(_moe, wgu=w_gu, bgu=b_gu, wd=w_down, bd=b_down, layer=i,
                                gpost=g_post_ffn[i][None])

        xm, h2, topi, topw, cnt = post(hyo_l, go_l, do_l, x=x, g1=g1, sc2=sc2, sh2=sh2, tm=tm_lat)
        x = moe(h2, topi, topw, cnt, xm=xm, g2=g2, tm=tm_lat)

        if ctx_out:
            go_c, do_c = attn(q_c, k_c, v_c, tq=tm_ctx)
            kf_ctx = filters(i, feats_ctx, mats_ctx)
            half_ctx = mats_ctx[0].shape[0] // 2
            hyo_c = _hyena(hy_c, hy_conv_w[i], hy_conv_b[i][None], hy_bias[i], kf_ctx, mats_ctx,
                           tm_ctx, _tile_rows(half_ctx, 512), tm_ctx)
            xcm, h2c, topic, topwc, cntc = post(hyo_c, go_c, do_c, x=xc, g1=cg1, sc2=csc2, sh2=csh2, tm=tm_ctx)
            xc = moe(h2c, topic, topwc, cntc, xm=xcm, g2=cg2, tm=tm_ctx)
    return x
```

```python
import functools
import math

import jax
import jax.numpy as jnp
from jax import lax
from jax.experimental import pallas as pl
from jax.experimental.pallas import tpu as pltpu

F32 = jnp.float32
BF16 = jnp.bfloat16
HIGHEST = lax.Precision.HIGHEST

HEAD_DIM = 64
ROPE_THETA = 10000.0
NORM_EPS = 1e-6
GRID_W = 64
HY_WIDTH = 256
HY_BANDS = 16
GQA_HEADS = 8
GQA_KV_HEADS = 2
GQA_GROUP = GQA_HEADS // GQA_KV_HEADS
DIFF_HEADS = 4
DIFF_QK_DIM = 32
DIFF_V_DIM = 64
TOP_K = 4
SWIGLU_LIMIT = 7.0
SWIGLU_ALPHA = 1.702

HY_COLS = 3 * HY_WIDTH
GQA_Q_COLS = GQA_HEADS * HEAD_DIM
DIFF_Q_COLS = DIFF_HEADS * 2 * DIFF_QK_DIM
GQA_KV_COLS = GQA_KV_HEADS * HEAD_DIM
DIFF_V_COLS = DIFF_HEADS * DIFF_V_DIM
Q_COLS = GQA_Q_COLS + DIFF_Q_COLS
KV_COLS = GQA_KV_COLS + DIFF_Q_COLS
Q_START = HY_COLS
K_START = Q_START + Q_COLS
IN_COLS = K_START + 2 * KV_COLS
V_HEAD_ROWS = HEAD_DIM + 16
VT_ROWS = (KV_COLS // HEAD_DIM) * V_HEAD_ROWS

LANES = 128
SUBLANES = 8
VMEM_LIMIT_BYTES = 56 * 1024 * 1024
NEG_BIG = -1e30
LOG2E = math.log2(math.e)
GAP_LIMIT = 60.0

_GQA_HEAD_ORDER = tuple(g + GQA_GROUP * j for g in range(GQA_GROUP) for j in range(GQA_KV_HEADS))


def _cparams(*sem):
    return pltpu.CompilerParams(dimension_semantics=sem, vmem_limit_bytes=VMEM_LIMIT_BYTES)


def _rms(x):
    return x * lax.rsqrt(jnp.mean(x * x, axis=-1, keepdims=True) + NORM_EPS)


def _ada_body(c_ref, w_ref, b_ref, o_ref):
    c = c_ref[...]
    s = c / (1.0 + jnp.exp(-c))
    o_ref[0] = jnp.dot(s.astype(BF16), w_ref[0].astype(BF16), preferred_element_type=F32) + b_ref[0]


def _ada(cc, w_ada, b_ada):
    n_layers, d, n = w_ada.shape
    rows = cc.shape[0]
    tn = 512
    return pl.pallas_call(
        _ada_body,
        grid=(n_layers, n // tn),
        in_specs=[pl.BlockSpec((rows, d), lambda l, j: (0, 0)),
                  pl.BlockSpec((1, d, tn), lambda l, j: (l, 0, j)),
                  pl.BlockSpec((1, 1, tn), lambda l, j: (l, 0, j))],
        out_specs=pl.BlockSpec((1, rows, tn), lambda l, j: (l, 0, j)),
        out_shape=jax.ShapeDtypeStruct((n_layers, rows, n), F32),
        compiler_params=_cparams("parallel", "parallel"),
    )(cc, w_ada, b_ada.reshape(n_layers, 1, n))


def _rope(xc, cos, sin, half, lane):
    fwd = pltpu.roll(xc, LANES - half, 1)
    bwd = pltpu.roll(xc, half, 1)
    sw = jnp.where((lane % (2 * half)) < half, fwd, bwd)
    return xc * cos + sw * sin


def _inproj_body(x_ref, sc_ref, sh_ref, g_ref, w_ref, qg_ref, kg_ref, bd_ref,
                 cg_ref, sg_ref, cd_ref, sd_ref, hy_ref, q_ref, k_ref, vt_ref):
    tm = x_ref.shape[1]
    h = _rms(x_ref[0]) * g_ref[...]
    h = h * (1.0 + sc_ref[0]) + sh_ref[0]
    p = jnp.dot(h.astype(BF16), w_ref[...], preferred_element_type=F32)
    hy_ref[0] = p[:, :HY_COLS]

    lane = lax.broadcasted_iota(jnp.int32, (tm, LANES), 1)
    cg, sg, cd, sd = cg_ref[...], sg_ref[...], cd_ref[...], sd_ref[...]
    bd = bd_ref[...]

    def head_norm(xc, gain):
        ms = jnp.dot(xc * xc, bd, precision=HIGHEST, preferred_element_type=F32) * (1.0 / HEAD_DIM)
        return xc * lax.rsqrt(ms + NORM_EPS) * gain

    for c in range(GQA_Q_COLS // LANES):
        xc = p[:, Q_START + c * LANES: Q_START + (c + 1) * LANES]
        xc = head_norm(xc, qg_ref[...])
        q_ref[0, :, c * LANES:(c + 1) * LANES] = _rope(xc, cg, sg, HEAD_DIM // 4, lane).astype(BF16)
    dscale = DIFF_QK_DIM ** -0.5 * LOG2E
    for c in range(DIFF_Q_COLS // LANES):
        o = Q_START + GQA_Q_COLS + c * LANES
        xc = p[:, o:o + LANES] * dscale
        q_ref[0, :, GQA_Q_COLS + c * LANES: GQA_Q_COLS + (c + 1) * LANES] = (
            _rope(xc, cd, sd, DIFF_QK_DIM // 4, lane).astype(BF16))

    gk = head_norm(p[:, K_START:K_START + GQA_KV_COLS], kg_ref[...])
    k_ref[0, :, :GQA_KV_COLS] = _rope(gk, cg, sg, HEAD_DIM // 4, lane).astype(BF16)
    dk0 = K_START + 2 * GQA_KV_COLS
    for c in range(DIFF_Q_COLS // LANES):
        xc = p[:, dk0 + c * LANES: dk0 + (c + 1) * LANES]
        k_ref[0, :, GQA_KV_COLS + c * LANES: GQA_KV_COLS + (c + 1) * LANES] = (
            _rope(xc, cd, sd, DIFF_QK_DIM // 4, lane).astype(BF16))
    kb = vt_ref.shape[3]
    for sblk in range(tm // kb):
        rows = slice(sblk * kb, (sblk + 1) * kb)
        vals = jnp.concatenate([p[rows, K_START + GQA_KV_COLS:K_START + 2 * GQA_KV_COLS],
                                p[rows, dk0 + DIFF_Q_COLS:]], axis=1)
        vals_t = vals.T.astype(BF16)
        ones = jnp.ones((V_HEAD_ROWS - HEAD_DIM, kb), BF16)
        for hd in range(KV_COLS // HEAD_DIM):
            vt_ref[0, sblk, hd * V_HEAD_ROWS:hd * V_HEAD_ROWS + HEAD_DIM, :] = vals_t[hd * HEAD_DIM:(hd + 1) * HEAD_DIM]
            vt_ref[0, sblk, hd * V_HEAD_ROWS + HEAD_DIM:(hd + 1) * V_HEAD_ROWS, :] = ones


def _inproj(x, scale, shift, g, w, qg, kg, bd, tables, tm, kb):
    b, t, d = x.shape
    row = lambda bi, i: (bi, i, 0)
    per_b = lambda bi, i: (bi, 0, 0)
    const = lambda bi, i: (0, 0)
    tab = pl.BlockSpec((tm, LANES), lambda bi, i: (i, 0))
    return pl.pallas_call(
        _inproj_body,
        grid=(b, t // tm),
        in_specs=[pl.BlockSpec((1, tm, d), row),
                  pl.BlockSpec((1, 1, d), per_b), pl.BlockSpec((1, 1, d), per_b),
                  pl.BlockSpec((1, d), const),
                  pl.BlockSpec((d, IN_COLS), const),
                  pl.BlockSpec((1, LANES), const), pl.BlockSpec((1, LANES), const),
                  pl.BlockSpec((LANES, LANES), const),
                  tab, tab, tab, tab],
        out_specs=[pl.BlockSpec((1, tm, HY_COLS), row), pl.BlockSpec((1, tm, Q_COLS), row),
                   pl.BlockSpec((1, tm, KV_COLS), row),
                   pl.BlockSpec((1, tm // kb, VT_ROWS, kb), lambda bi, i: (bi, i, 0, 0))],
        out_shape=[jax.ShapeDtypeStruct((b, t, HY_COLS), F32), jax.ShapeDtypeStruct((b, t, Q_COLS), BF16),
                   jax.ShapeDtypeStruct((b, t, KV_COLS), BF16),
                   jax.ShapeDtypeStruct((b, t // kb, VT_ROWS, kb), BF16)],
        compiler_params=_cparams("parallel", "parallel"),
    )(x, scale, shift, g, w, qg, kg, bd, *tables)


def _filter_body(f_ref, w1_ref, b1_ref, w2_ref, b2_ref, w3_ref, fr_ref, dc_ref, o_ref):
    n = f_ref.shape[0]
    fr = fr_ref[...]
    h = jnp.sin(fr * (jnp.dot(f_ref[...], w1_ref[...], precision=HIGHEST, preferred_element_type=F32) + b1_ref[...]))
    h = jnp.sin(fr * (jnp.dot(h, w2_ref[...], precision=HIGHEST, preferred_element_type=F32) + b2_ref[...]))
    k = jnp.dot(h, w3_ref[...], precision=HIGHEST, preferred_element_type=F32)
    t = lax.broadcasted_iota(jnp.int32, k.shape, 0)
    off = jnp.abs(t - n // 2).astype(F32) * (2.0 / n)
    k = k * jnp.exp(-off * jnp.abs(dc_ref[...]))
    k = k / jnp.sum(jnp.abs(k), axis=0, keepdims=True)
    o_ref[...] = k.astype(BF16)


def _hyena_filters(feats, w1p, b1, w2, b2, w3, freq, decay):
    n = feats.shape[0]
    hid = w2.shape[0]
    cols = w3.shape[1]
    const = lambda j: (0, 0)
    return pl.pallas_call(
        _filter_body,
        grid=(cols // LANES,),
        in_specs=[pl.BlockSpec((n, LANES), const), pl.BlockSpec((LANES, hid), const),
                  pl.BlockSpec((1, hid), const), pl.BlockSpec((hid, hid), const), pl.BlockSpec((1, hid), const),
                  pl.BlockSpec((hid, LANES), lambda j: (0, j)), pl.BlockSpec((1, hid), const),
                  pl.BlockSpec((1, LANES), lambda j: (0, j))],
        out_specs=pl.BlockSpec((n, LANES), lambda j: (0, j)),
        out_shape=jax.ShapeDtypeStruct((n, cols), BF16),
        compiler_params=_cparams("parallel"),
    )(feats, w1p, b1, w2, b2, w3, freq, decay)


def _hyprep_body(p_ref, prev_ref, next_ref, w_ref, b_ref, v_ref, vb_ref, x1_ref, x2_ref, buf):
    i = pl.program_id(1)
    n = pl.num_programs(1)
    tm = p_ref.shape[1]
    buf[SUBLANES:tm + SUBLANES, :] = p_ref[0]
    buf[SUBLANES - 1:SUBLANES, :] = jnp.where(i > 0, prev_ref[0][SUBLANES - 1:SUBLANES, :], 0.0)
    buf[tm + SUBLANES:tm + SUBLANES + 1, :] = jnp.where(i < n - 1, next_ref[0][0:1, :], 0.0)
    w = w_ref[...]
    y = (buf[SUBLANES - 1:tm + SUBLANES - 1, :] * w[0:1] + buf[SUBLANES:tm + SUBLANES, :] * w[1:2]
         + buf[SUBLANES + 1:tm + SUBLANES + 1, :] * w[2:3] + b_ref[...])
    v_ref[...] = y[:, :HY_WIDTH]
    vb_ref[...] = y[:, :HY_WIDTH].astype(BF16)
    x1_ref[...] = y[:, HY_WIDTH:2 * HY_WIDTH]
    x2_ref[...] = y[:, 2 * HY_WIDTH:]


def _hyprep(p_hy, conv_w, conv_b, tm):
    b, t, _ = p_hy.shape
    nb8 = t // SUBLANES
    r8 = tm // SUBLANES
    tmaj = pl.BlockSpec((tm, HY_WIDTH), lambda bi, i: (i, bi))
    shp = lambda dt: jax.ShapeDtypeStruct((t, b * HY_WIDTH), dt)
    return pl.pallas_call(
        _hyprep_body,
        grid=(b, t // tm),
        in_specs=[pl.BlockSpec((1, tm, HY_COLS), lambda bi, i: (bi, i, 0)),
                  pl.BlockSpec((1, SUBLANES, HY_COLS), lambda bi, i: (bi, jnp.maximum(i * r8 - 1, 0), 0)),
                  pl.BlockSpec((1, SUBLANES, HY_COLS), lambda bi, i: (bi, jnp.minimum((i + 1) * r8, nb8 - 1), 0)),
                  pl.BlockSpec((3, HY_COLS), lambda bi, i: (0, 0)),
                  pl.BlockSpec((1, HY_COLS), lambda bi, i: (0, 0))],
        out_specs=[tmaj, tmaj, tmaj, tmaj],
        out_shape=[shp(F32), shp(BF16), shp(F32), shp(F32)],
        scratch_shapes=[pltpu.VMEM((tm + 2 * SUBLANES, HY_COLS), F32)],
        compiler_params=_cparams("parallel", "parallel"),
    )(p_hy, p_hy, p_hy, conv_w, conv_b)


def _dft_mats(n_t):
    n_fft = 3 * n_t // 2
    half = n_fft // 2
    step = 2.0 * math.pi / n_fft

    def tables(rows, first, width):
        r = jnp.arange(rows, dtype=jnp.int32)
        c = jnp.arange(width, dtype=jnp.int32)
        base = ((r[:, None] * c[None, :]) % n_fft).astype(F32) * step
        start = ((first[:, None] * c[None, :]) % n_fft).astype(F32) * step
        return jnp.cos(base), jnp.sin(base), jnp.cos(start)[:, None, :], jnp.sin(start)[:, None, :]

    tm = _tile_rows(half, 256)
    nblk = half // tm
    cb, sb, c0, s0 = tables(tm, jnp.arange(nblk, dtype=jnp.int32) * tm, n_t)
    tile = pl.BlockSpec((tm, n_t), lambda p, i: (0, 0))
    vec = pl.BlockSpec((1, 1, n_t), lambda p, i: (i, 0, 0))
    fwd = pl.pallas_call(
        _dftgen_fwd_body,
        grid=(2, nblk),
        in_specs=[tile, tile, vec, vec],
        out_specs=pl.BlockSpec((tm, n_t), lambda p, i: (p * nblk + i, 0)),
        out_shape=jax.ShapeDtypeStruct((n_fft, n_t), BF16),
        compiler_params=_cparams("parallel", "parallel"),
    )(cb, sb, c0, s0)

    tm = _tile_rows(n_t, 256)
    nblk = n_t // tm
    cb, sb, c0, s0 = tables(tm, jnp.arange(nblk, dtype=jnp.int32) * tm + n_t // 2, half)
    tile = pl.BlockSpec((tm, half), lambda i: (0, 0))
    vec = pl.BlockSpec((1, 1, half), lambda i: (i, 0, 0))
    out = pl.BlockSpec((tm, half), lambda i: (i, 0))
    g_re, g_im = pl.pallas_call(
        functools.partial(_dftgen_inv_body, first_sample=n_t // 2),
        grid=(nblk,),
        in_specs=[tile, tile, vec, vec],
        out_specs=[out, out],
        out_shape=[jax.ShapeDtypeStruct((n_t, half), BF16)] * 2,
        compiler_params=_cparams("parallel"),
    )(cb, sb, c0, s0)
    return fwd, g_re, g_im


def _dftgen_fwd_body(cb_ref, sb_ref, c0_ref, s0_ref, o_ref):
    cb, sb, c0, s0 = cb_ref[...], sb_ref[...], c0_ref[0], s0_ref[0]
    is_im = pl.program_id(0) == 1
    re = cb * c0 - sb * s0
    im = -(sb * c0 + cb * s0)
    col = lax.broadcasted_iota(jnp.int32, cb.shape, 1)
    row0 = (lax.broadcasted_iota(jnp.int32, cb.shape, 0) == 0) & (pl.program_id(1) == 0)
    im = jnp.where(row0, (1 - 2 * (col % 2)).astype(F32), im)
    o_ref[...] = jnp.where(is_im, im, re).astype(BF16)


def _dftgen_inv_body(cb_ref, sb_ref, c0_ref, s0_ref, re_ref, im_ref, *, first_sample):
    cb, sb, c0, s0 = cb_ref[...], sb_ref[...], c0_ref[0], s0_ref[0]
    re = 2.0 * (cb * c0 - sb * s0)
    im = -2.0 * (sb * c0 + cb * s0)
    col0 = lax.broadcasted_iota(jnp.int32, cb.shape, 1) == 0
    n = lax.broadcasted_iota(jnp.int32, cb.shape, 0) + pl.program_id(0) * cb.shape[0] + first_sample
    re_ref[...] = jnp.where(col0, 1.0, re).astype(BF16)
    im_ref[...] = jnp.where(col0, (1 - 2 * (n % 2)).astype(F32), im).astype(BF16)


def _dft_fwd_body(fre_ref, fim_ref, u_ref, zre_ref, zim_ref):
    u = u_ref[...]
    zre_ref[...] = jnp.dot(fre_ref[...], u, preferred_element_type=F32)
    zim_ref[...] = jnp.dot(fim_ref[...], u, preferred_element_type=F32)


def _dft_fwd_mul_body(fre_ref, fim_ref, u_ref, kre_ref, kim_ref, zre_ref, zim_ref):
    u = u_ref[...]
    xr = jnp.dot(fre_ref[...], u, preferred_element_type=F32)
    xi = jnp.dot(fim_ref[...], u, preferred_element_type=F32)
    kre, kim = kre_ref[...], kim_ref[...]
    row0 = (lax.broadcasted_iota(jnp.int32, xr.shape, 0) == 0) & (pl.program_id(0) == 0)
    a = xr * kre
    bb = xi * kim
    zre_ref[...] = (a - jnp.where(row0, 0.0, bb)).astype(zre_ref.dtype)
    zim_ref[...] = jnp.where(row0, bb, xr * kim + xi * kre).astype(zim_ref.dtype)


def _dft_fwd(fwd, u, tm, kf=None, order=0):
    n_fft, n_t = fwd.shape
    half = n_fft // 2
    cols = u.shape[1]
    nblk = half // tm
    tn = HY_WIDTH
    in_specs = [pl.BlockSpec((tm, n_t), lambda i, j: (i, 0)),
                pl.BlockSpec((tm, n_t), lambda i, j: (i + nblk, 0)),
                pl.BlockSpec((n_t, tn), lambda i, j: (0, j))]
    args = [fwd, fwd, u]
    if kf is None:
        body, odt = _dft_fwd_body, F32
    else:
        body, odt = _dft_fwd_mul_body, BF16
        in_specs += [pl.BlockSpec((tm, tn), lambda i, j: (i, order)), pl.BlockSpec((tm, tn), lambda i, j: (i, order))]
        args += [kf[0], kf[1]]
    out = pl.BlockSpec((tm, tn), lambda i, j: (i, j))
    return pl.pallas_call(
        body,
        grid=(nblk, cols // tn),
        in_specs=in_specs,
        out_specs=[out, out],
        out_shape=[jax.ShapeDtypeStruct((half, cols), odt)] * 2,
        compiler_params=_cparams("parallel", "parallel"),
    )(*args)


def _dft_inv_body(gre_ref, gim_ref, zre_ref, zim_ref, u_ref, x_ref, bias_ref, *out_refs, inv_n):
    y = (jnp.dot(gre_ref[...], zre_ref[...], preferred_element_type=F32)
         + jnp.dot(gim_ref[...], zim_ref[...], preferred_element_type=F32)) * inv_n
    o = x_ref[...] * (y + u_ref[...] * bias_ref[...])
    if len(out_refs) == 2:
        out_refs[0][...] = o
        out_refs[1][...] = o.astype(BF16)
    else:
        out_refs[0][0] = o.astype(BF16)


def _dft_inv(g_re, g_im, zre, zim, u, xg, bias, tm, batch, last):
    n_t, half = g_re.shape
    cols = u.shape[1]
    tn = HY_WIDTH
    gsp = pl.BlockSpec((tm, half), lambda i, j: (i, 0))
    zsp = pl.BlockSpec((half, tn), lambda i, j: (0, j))
    tsp = pl.BlockSpec((tm, tn), lambda i, j: (i, j))
    if last:
        out_specs = [pl.BlockSpec((1, tm, tn), lambda i, j: (j, i, 0))]
        out_shape = [jax.ShapeDtypeStruct((batch, n_t, tn), BF16)]
    else:
        out_specs = [tsp, tsp]
        out_shape = [jax.ShapeDtypeStruct((n_t, cols), F32), jax.ShapeDtypeStruct((n_t, cols), BF16)]
    return pl.pallas_call(
        functools.partial(_dft_inv_body, inv_n=1.0 / (2 * half)),
        grid=(n_t // tm, cols // tn),
        in_specs=[gsp, gsp, zsp, zsp, tsp, tsp, pl.BlockSpec((1, tn), lambda i, j: (0, 0))],
        out_specs=out_specs,
        out_shape=out_shape,
        compiler_params=_cparams("parallel", "parallel"),
    )(g_re, g_im, zre, zim, u, xg, bias)


def _hyena(p_hy, conv_w, conv_b, hy_bias, kf, mats, tm_prep, tm_f, tm_t):
    b = p_hy.shape[0]
    fwd, g_re, g_im = mats
    v, vb, x1, x2 = _hyprep(p_hy, conv_w, conv_b, tm_prep)
    zre, zim = _dft_fwd(fwd, vb, tm_f, kf, 0)
    y1, y1b = _dft_inv(g_re, g_im, zre, zim, v, x1, hy_bias[0:1], tm_t, b, False)
    zre, zim = _dft_fwd(fwd, y1b, tm_f, kf, 1)
    (hy,) = _dft_inv(g_re, g_im, zre, zim, y1, x2, hy_bias[1:2], tm_t, b, True)
    return hy


def _attn_body(q_ref, k_ref, vt_ref, lam_ref, sub_ref, go_ref, do_ref, *, lam_init, group, ahead):
    tq = q_ref.shape[1]
    nkb, _, kb = vt_ref.shape[1:]
    lp = lam_ref[...]
    lam = (jnp.exp(jnp.sum(lp[0:1] * lp[1:2], axis=1, keepdims=True))
           - jnp.exp(jnp.sum(lp[2:3] * lp[3:4], axis=1, keepdims=True)) + lam_init)

    def attend(maps):
        def score(i, qm, kcol, kw):
            off = pl.multiple_of(i * kb, kb)
            return lax.dot_general(k_ref[0, pl.ds(off, kb), kcol:kcol + kw], qm, (((1,), (1,)), ((), ())),
                                   preferred_element_type=F32)

        def update(i, s, m, acc, vrow):
            vblk = vt_ref[0, i, vrow:vrow + V_HEAD_ROWS, :]
            m_new = jnp.maximum(m, jnp.max(s, axis=0, keepdims=True))
            alpha = jnp.exp2(m - m_new)
            p = jnp.exp2((s - m_new).astype(BF16))
            return m_new, alpha * acc + jnp.dot(vblk, p, preferred_element_type=F32), None

        def update_lagged(i, s, m, acc, vrow):
            vblk = vt_ref[0, i, vrow:vrow + V_HEAD_ROWS, :]
            p = jnp.exp2((s - m).astype(BF16))
            pv = jnp.dot(vblk, p, preferred_element_type=F32)
            m_new = jnp.maximum(m, jnp.max(s, axis=0, keepdims=True))
            return m_new, (acc + pv) * jnp.exp2(m - m_new), m_new - m

        def block(i, state, step):
            scores, new, rise = {}, [], jnp.zeros((1, tq), F32)
            for j in range(len(maps) + ahead):
                if j < len(maps):
                    scores[j] = score(i, *maps[j][:3])
                if j >= ahead:
                    jj = j - ahead
                    m, acc, gap = step(i, scores.pop(jj), *state[jj], maps[jj][3])
                    new.append((m, acc))
                    rise = rise if gap is None else jnp.maximum(rise, gap)
            return tuple(new), rise

        def finish(state):
            return tuple(acc[:HEAD_DIM] / acc[HEAD_DIM:HEAD_DIM + 1] for (_, acc) in state)

        init = tuple((jnp.full((1, tq), NEG_BIG, F32), jnp.zeros((V_HEAD_ROWS, tq), F32)) for _ in maps)
        first, _ = block(0, init, update)

        def lagged_body(i, carry):
            state, rise = carry
            state, r = block(i, state, update_lagged)
            return state, jnp.maximum(rise, r)
        state, rise = lax.fori_loop(1, nkb, lagged_body, (first, jnp.zeros((1, tq), F32)))

        def redo():
            return finish(lax.fori_loop(1, nkb, lambda i, st: block(i, st, update)[0], first))
        return list(lax.cond(jnp.max(rise) > GAP_LIMIT, redo, lambda: finish(state)))

    lane = lax.broadcasted_iota(jnp.int32, (tq, LANES), 1)
    lane_d = lax.broadcasted_iota(jnp.int32, (tq, DIFF_Q_COLS), 1)
    dq = q_ref[0, :, GQA_Q_COLS:]
    maps = []
    for c in range(GQA_Q_COLS // LANES):
        qc = q_ref[0, :, c * LANES:(c + 1) * LANES]
        for j in range(GQA_KV_HEADS):
            maps.append((jnp.where((lane // HEAD_DIM) == j, qc, jnp.zeros_like(qc)), 0, GQA_KV_COLS, j * V_HEAD_ROWS))
    for h in range(DIFF_HEADS):
        for cmap in range(2):
            mi = 2 * h + cmap
            half = (mi * DIFF_QK_DIM) // LANES
            dqh = dq[:, half * LANES:(half + 1) * LANES]
            qm = jnp.where((lane // DIFF_QK_DIM) == mi - half * (LANES // DIFF_QK_DIM), dqh, jnp.zeros_like(dqh))
            maps.append((qm, GQA_KV_COLS + half * LANES, LANES, (GQA_KV_HEADS + h) * V_HEAD_ROWS))
    outs = []
    for g in range(0, len(maps), group):
        outs += attend(maps[g:g + group])

    n_gqa = GQA_HEADS
    for c in range(GQA_Q_COLS // LANES):
        pair = jnp.concatenate(outs[GQA_KV_HEADS * c:GQA_KV_HEADS * (c + 1)], axis=0)
        go_ref[0, :, c * LANES:(c + 1) * LANES] = pair.T.astype(BF16)
    sub = sub_ref[...] * (1.0 - lam_init)
    heads = []
    for h in range(DIFF_HEADS):
        d = outs[n_gqa + 2 * h] - lam * outs[n_gqa + 2 * h + 1]
        ms = jnp.mean(d * d, axis=0, keepdims=True)
        heads.append(d * lax.rsqrt(ms + NORM_EPS) * sub)
    do_ref[0] = jnp.concatenate(heads, axis=0).T.astype(BF16)


def _attention(q, k, vt, lam_p, sub, lam_init, tq, group=16, ahead=10):
    b, t, _ = q.shape
    lk = k.shape[1]
    row = lambda bi, i: (bi, i, 0)
    sub_t = jnp.broadcast_to(sub.reshape(DIFF_V_DIM, 1), (DIFF_V_DIM, tq))
    return pl.pallas_call(
        functools.partial(_attn_body, lam_init=lam_init, group=group, ahead=ahead),
        grid=(b, t // tq),
        in_specs=[pl.BlockSpec((1, tq, Q_COLS), row),
                  pl.BlockSpec((1, lk, KV_COLS), lambda bi, i: (bi, 0, 0)),
                  pl.BlockSpec((1,) + vt.shape[1:], lambda bi, i: (bi, 0, 0, 0)),
                  pl.BlockSpec(lam_p.shape, lambda bi, i: (0, 0)),
                  pl.BlockSpec((DIFF_V_DIM, tq), lambda bi, i: (0, 0))],
        out_specs=[pl.BlockSpec((1, tq, GQA_Q_COLS), row), pl.BlockSpec((1, tq, DIFF_V_COLS), row)],
        out_shape=[jax.ShapeDtypeStruct((b, t, GQA_Q_COLS), BF16), jax.ShapeDtypeStruct((b, t, DIFF_V_COLS), BF16)],
        compiler_params=_cparams("parallel", "parallel"),
    )(q, k, vt, lam_p, sub_t)


def _outproj_body(hy_ref, go_ref, do_ref, w_ref, x_ref, g1_ref, gpost_ref, gpre_ref, sc_ref, sh_ref,
                  wr_ref, br_ref, xm_ref, h2_ref, topi_ref, topw_ref, cnt_ref):
    y = (jnp.dot(hy_ref[0], w_ref[:HY_WIDTH], preferred_element_type=F32)
         + jnp.dot(go_ref[0], w_ref[HY_WIDTH:HY_WIDTH + GQA_Q_COLS], preferred_element_type=F32)
         + jnp.dot(do_ref[0], w_ref[HY_WIDTH + GQA_Q_COLS:], preferred_element_type=F32))
    xm = x_ref[0] + g1_ref[0] * (_rms(y) * gpost_ref[...])
    xm_ref[0] = xm
    h2 = _rms(xm) * gpre_ref[...] * (1.0 + sc_ref[0]) + sh_ref[0]
    h2_ref[0] = h2

    h_hi = h2.astype(BF16)
    h_lo = (h2 - h_hi.astype(F32)).astype(BF16)
    logits = (jnp.dot(h_hi, wr_ref[0], preferred_element_type=F32) + jnp.dot(h_lo, wr_ref[0], preferred_element_type=F32)
              + jnp.dot(h_hi, wr_ref[1], preferred_element_type=F32) + br_ref[...])
    lane = lax.broadcasted_iota(jnp.int32, logits.shape, 1)
    topi = jnp.zeros(logits.shape, jnp.int32)
    topw = jnp.zeros_like(logits)
    chosen = jnp.zeros_like(logits)
    denom = jnp.zeros((logits.shape[0], 1), F32)
    top = None
    for k in range(TOP_K):
        mx = jnp.max(logits, axis=1, keepdims=True)
        first = jnp.min(jnp.where(logits == mx, lane, LANES), axis=1, keepdims=True)
        sel = lane == first
        top = mx if top is None else top
        e = jnp.exp(mx - top)
        topi = jnp.where(lane == k, first, topi)
        topw = jnp.where(lane == k, e, topw)
        chosen = jnp.where(sel, 1.0, chosen)
        denom = denom + e
        logits = jnp.where(sel, NEG_BIG, logits)
    topi_ref[0] = topi[:, :TOP_K]
    topw_ref[0] = (topw / denom)[:, :TOP_K]

    @pl.when((pl.program_id(0) == 0) & (pl.program_id(1) == 0))
    def _():
        cnt_ref[...] = jnp.zeros_like(cnt_ref)
    cnt_ref[...] += jnp.sum(chosen, axis=0, keepdims=True)


def _outproj(hy, go, do, w_out, x, g1, gpost, gpre, sc2, sh2, wr, br, tm):
    b, t, d = x.shape
    row = lambda bi, i: (bi, i, 0)
    per_b = lambda bi, i: (bi, 0, 0)
    const = lambda bi, i: (0, 0)
    vec = pl.BlockSpec((1, 1, d), per_b)
    return pl.pallas_call(
        _outproj_body,
        grid=(b, t // tm),
        in_specs=[pl.BlockSpec((1, tm, HY_WIDTH), row), pl.BlockSpec((1, tm, GQA_Q_COLS), row),
                  pl.BlockSpec((1, tm, DIFF_V_COLS), row), pl.BlockSpec(w_out.shape, const),
                  pl.BlockSpec((1, tm, d), row), vec, pl.BlockSpec((1, d), const), pl.BlockSpec((1, d), const),
                  vec, vec, pl.BlockSpec((2, d, LANES), lambda bi, i: (0, 0, 0)), pl.BlockSpec((1, LANES), const)],
        out_specs=[pl.BlockSpec((1, tm, d), row), pl.BlockSpec((1, tm, d), row),
                   pl.BlockSpec((1, tm, TOP_K), row), pl.BlockSpec((1, tm, TOP_K), row),
                   pl.BlockSpec((SUBLANES, LANES), const)],
        out_shape=[jax.ShapeDtypeStruct((b, t, d), F32), jax.ShapeDtypeStruct((b, t, d), F32),
                   jax.ShapeDtypeStruct((b, t, TOP_K), jnp.int32), jax.ShapeDtypeStruct((b, t, TOP_K), F32),
                   jax.ShapeDtypeStruct((SUBLANES, LANES), F32)],
        compiler_params=_cparams("arbitrary", "arbitrary"),
    )(hy, go, do, w_out, x, g1, gpost, gpre, sc2, sh2, wr, br)


EXPERT_TILE = 256


def _positions_body(topi_ref, base_ref, pos_ref, carry_ref):
    tm = topi_ref.shape[0]

    @pl.when(pl.program_id(0) == 0)
    def _():
        carry_ref[...] = jnp.zeros_like(carry_ref)

    topi = topi_ref[...]
    lane = lax.broadcasted_iota(jnp.int32, (tm, LANES), 1)
    hits = [lane == topi[:, k:k + 1] for k in range(TOP_K)]
    chosen = jnp.zeros((tm, LANES), F32)
    for hit in hits:
        chosen = jnp.where(hit, 1.0, chosen)
    r = lax.broadcasted_iota(jnp.int32, (tm, tm), 0)
    c = lax.broadcasted_iota(jnp.int32, (tm, tm), 1)
    before = jnp.where(c < r, 1.0, 0.0).astype(BF16)
    rank = jnp.dot(before, chosen.astype(BF16), preferred_element_type=F32) + carry_ref[0:1, :] + base_ref[...]
    pos = jnp.zeros((tm, LANES), F32)
    for k, hit in enumerate(hits):
        pos = jnp.where(lane == k, jnp.sum(jnp.where(hit, rank, 0.0), axis=1, keepdims=True), pos)
    pos_ref[...] = pos[:, :TOP_K].astype(jnp.int32)
    carry_ref[...] += jnp.sum(chosen, axis=0, keepdims=True)


def _positions(topi, base, tm):
    n = topi.shape[0]
    return pl.pallas_call(
        _positions_body,
        grid=(n // tm,),
        in_specs=[pl.BlockSpec((tm, TOP_K), lambda i: (i, 0)), pl.BlockSpec((1, LANES), lambda i: (0, 0))],
        out_specs=pl.BlockSpec((tm, TOP_K), lambda i: (i, 0)),
        out_shape=jax.ShapeDtypeStruct((n, TOP_K), jnp.int32),
        scratch_shapes=[pltpu.VMEM((SUBLANES, LANES), F32)],
        compiler_params=_cparams("arbitrary"),
    )(topi, base)


def _row_copy(src, src_row, dst, dst_row, sem):
    return pltpu.make_async_copy(src.at[pl.ds(src_row, 1)], dst.at[pl.ds(dst_row, 1)], sem)


def _scatter_body(seg_ref, pos_ref, h_ref, xs_ref, zero_ref, sem, zsem):
    i = pl.program_id(0)
    n_groups = h_ref.shape[0]
    n_exp = seg_ref.shape[1]

    @pl.when(i == 0)
    def _():
        zero_ref[...] = jnp.zeros_like(zero_ref)
        for wait in (False, True):
            def per_expert(e, _):
                def per_row(r, _):
                    cp = _row_copy(zero_ref, 0, xs_ref, seg_ref[0, e] + r, zsem)
                    cp.wait() if wait else cp.start()
                    return 0
                return lax.fori_loop(seg_ref[1, e], seg_ref[2, e], per_row, 0)
            lax.fori_loop(0, n_exp, per_expert, 0)

    for wait in (False, True):
        def per_group(g, _):
            p0 = g * (SUBLANES * TOP_K)
            for j in range(SUBLANES):
                for k in range(TOP_K):
                    cp = pltpu.make_async_copy(h_ref.at[g, pl.ds(j, 1)],
                                               xs_ref.at[pl.ds(pos_ref[p0 + j * TOP_K + k], 1)], sem)
                    cp.wait() if wait else cp.start(priority=k % 2)
            return 0
        lax.fori_loop(0, n_groups, per_group, 0)


def _scatter(seg, pos_flat, h2, n_rows, ts):
    n, d = h2.shape
    h2 = h2.reshape(n // SUBLANES, SUBLANES, d)
    return pl.pallas_call(
        _scatter_body,
        grid_spec=pltpu.PrefetchScalarGridSpec(
            num_scalar_prefetch=1,
            grid=(n // ts,),
            in_specs=[pl.BlockSpec((ts * TOP_K,), lambda i, seg: (i,), memory_space=pltpu.SMEM),
                      pl.BlockSpec((ts // SUBLANES, SUBLANES, d), lambda i, seg: (i, 0, 0))],
            out_specs=pl.BlockSpec(memory_space=pl.ANY),
            scratch_shapes=[pltpu.VMEM((SUBLANES, d), F32), pltpu.SemaphoreType.DMA(()), pltpu.SemaphoreType.DMA(())],
        ),
        out_shape=jax.ShapeDtypeStruct((n_rows, d), F32),
        compiler_params=_cparams("arbitrary"),
    )(seg, pos_flat, h2)


def _expert_body(te_ref, na_ref, x_ref, wgu_ref, bgu_ref, wd_ref, bd_ref, y_ref, wgu_b, wd_b):
    i = pl.program_id(0)
    ff = wd_ref.shape[1]
    active = i < na_ref[0]
    new_expert = (i == 0) | (te_ref[i] != te_ref[jnp.maximum(i - 1, 0)])

    @pl.when(active & new_expert)
    def _():
        wgu_b[...] = wgu_ref[0].astype(BF16)
        wd_b[...] = wd_ref[0].astype(BF16)

    @pl.when(active)
    def _():
        gu = jnp.dot(x_ref[...].astype(BF16), wgu_b[...], preferred_element_type=F32) + bgu_ref[0]
        gate = jnp.minimum(gu[:, :ff], SWIGLU_LIMIT)
        up = jnp.clip(gu[:, ff:], -SWIGLU_LIMIT, SWIGLU_LIMIT)
        act = (up + 1.0) * gate / (1.0 + jnp.exp(-SWIGLU_ALPHA * gate))
        y_ref[...] = jnp.dot(act.astype(BF16), wd_b[...], preferred_element_type=F32) + bd_ref[0]


def _experts(tile_expert, n_active, xs, wgu, bgu, wd, bd, layer):
    n_rows, d = xs.shape
    depth, n_exp, _, ff2 = wgu.shape
    ff = ff2 // 2
    tile = lambda i, te, na: (jnp.minimum(i, na[0] - 1), 0)
    first_expert = layer * n_exp
    wsel = lambda i, te, na: (first_expert + te[jnp.minimum(i, na[0] - 1)], 0, 0)
    n_exp = depth * n_exp
    wgu = wgu.reshape(n_exp, d, ff2)
    wd = wd.reshape(n_exp, ff, d)
    return pl.pallas_call(
        _expert_body,
        grid_spec=pltpu.PrefetchScalarGridSpec(
            num_scalar_prefetch=2,
            grid=(n_rows // EXPERT_TILE,),
            in_specs=[pl.BlockSpec((EXPERT_TILE, d), tile),
                      pl.BlockSpec((1, d, ff2), wsel), pl.BlockSpec((1, 1, ff2), wsel),
                      pl.BlockSpec((1, ff, d), wsel), pl.BlockSpec((1, 1, d), wsel)],
            out_specs=pl.BlockSpec((EXPERT_TILE, d), tile),
            scratch_shapes=[pltpu.VMEM((d, ff2), BF16), pltpu.VMEM((ff, d), BF16)],
        ),
        out_shape=jax.ShapeDtypeStruct((n_rows, d), F32),
        compiler_params=_cparams("arbitrary"),
    )(tile_expert, n_active, xs, wgu, bgu.reshape(n_exp, 1, ff2), wd, bd.reshape(n_exp, 1, d))


def _combine_body(pos_ref, w_ref, x_ref, g2_ref, gpost_ref, ys_ref, o_ref, buf, sem):
    tc, d = x_ref.shape
    for wait in (False, True):
        def per_group(g, _):
            p0 = g * (SUBLANES * TOP_K)
            for j in range(SUBLANES):
                for k in range(TOP_K):
                    cp = pltpu.make_async_copy(ys_ref.at[pl.ds(pos_ref[p0 + j * TOP_K + k], 1)],
                                               buf.at[k, g, pl.ds(j, 1)], sem)
                    cp.wait() if wait else cp.start(priority=k % 2)
            return 0
        lax.fori_loop(0, tc // SUBLANES, per_group, 0)
    w = w_ref[...]
    y = w[:, 0:1] * buf[0].reshape(tc, d)
    for k in range(1, TOP_K):
        y = y + w[:, k:k + 1] * buf[k].reshape(tc, d)
    o_ref[...] = x_ref[...] + g2_ref[0] * (_rms(y) * gpost_ref[...])


def _combine(pos_flat, topw, xm, g2, gpost, ys, tokens_per_batch, tc):
    n, d = xm.shape
    per_tile = tokens_per_batch // tc
    tok = lambda i: (i, 0)
    return pl.pallas_call(
        _combine_body,
        grid=(n // tc,),
        in_specs=[pl.BlockSpec((tc * TOP_K,), lambda i: (i,), memory_space=pltpu.SMEM),
                  pl.BlockSpec((tc, TOP_K), tok), pl.BlockSpec((tc, d), tok),
                  pl.BlockSpec((1, 1, d), lambda i: (i // per_tile, 0, 0)), pl.BlockSpec((1, d), lambda i: (0, 0)),
                  pl.BlockSpec(memory_space=pl.ANY)],
        out_specs=pl.BlockSpec((tc, d), tok),
        out_shape=jax.ShapeDtypeStruct((n, d), F32),
        scratch_shapes=[pltpu.VMEM((TOP_K, tc // SUBLANES, SUBLANES, d), F32), pltpu.SemaphoreType.DMA(())],
        compiler_params=_cparams("arbitrary"),
    )(pos_flat, topw, xm, g2, gpost, ys)


def _moe(h2, topi, topw, counts, wgu, bgu, wd, bd, layer, xm, g2, gpost, tm):
    b, t, d = xm.shape
    n_exp = wgu.shape[1]
    n = b * t
    cnt = counts[0, :n_exp].astype(jnp.int32)
    padded = (cnt + EXPERT_TILE - 1) // EXPERT_TILE * EXPERT_TILE
    ends = jnp.cumsum(padded)
    first = ends - padded
    n_rows = n * TOP_K + n_exp * EXPERT_TILE
    tile_start = jnp.arange(n_rows // EXPERT_TILE, dtype=jnp.int32) * EXPERT_TILE
    tile_expert = jnp.minimum(jnp.sum(tile_start[:, None] >= ends[None, :], axis=1), n_exp - 1).astype(jnp.int32)
    n_active = (ends[-1:] // EXPERT_TILE).astype(jnp.int32)
    base = jnp.zeros((1, LANES), F32).at[0, :n_exp].set(first.astype(F32))
    seg = jnp.stack([first, cnt, padded]).astype(jnp.int32)

    pos = _positions(topi.reshape(n, TOP_K), base, tm)
    pos_flat = pos.reshape(n * TOP_K)
    xs = _scatter(seg, pos_flat, h2.reshape(n, d), n_rows, tm)
    ys = _experts(tile_expert, n_active, xs, wgu, bgu, wd, bd, layer)
    out = _combine(pos_flat, topw.reshape(n, TOP_K), xm.reshape(n, d), g2, gpost, ys, t, min(tm, 256))
    return out.reshape(b, t, d)


def _rope_tables(n_tok, dim, use_pos):
    if not use_pos:
        return jnp.ones((n_tok, LANES), F32), jnp.zeros((n_tok, LANES), F32)
    half = dim // 4
    t = jnp.arange(n_tok, dtype=jnp.int32)
    row = (t // GRID_W).astype(F32)
    col = (t % GRID_W).astype(F32)
    inv_freq = ROPE_THETA ** (-jnp.arange(half, dtype=F32) / half)
    a_row = row[:, None] * inv_freq[None, :]
    a_col = col[:, None] * inv_freq[None, :]
    cos = jnp.concatenate([jnp.cos(a_row)] * 2 + [jnp.cos(a_col)] * 2, axis=1)
    sin = jnp.concatenate([-jnp.sin(a_row), jnp.sin(a_row), -jnp.sin(a_col), jnp.sin(a_col)], axis=1)
    reps = LANES // dim
    return jnp.tile(cos, (1, reps)), jnp.tile(sin, (1, reps))


def _hyena_feats(n_t):
    t = jnp.arange(n_t, dtype=F32)
    bands = jnp.linspace(1e-4, HY_BANDS - 1, HY_BANDS, dtype=F32)
    w = (2.0 * math.pi / n_t) * t
    feats = jnp.concatenate([(t / n_t)[:, None], jnp.cos(w[:, None] * bands), -jnp.sin(w[:, None] * bands)], axis=-1)
    return jnp.pad(feats, ((0, 0), (0, LANES - feats.shape[1])))


def _tile_rows(t, cap):
    return cap if t % cap == 0 else t


def kernel(x, c, ctx, c_ctx, w_ada, b_ada, g_pre_mix, g_post_mix, g_pre_ffn, g_post_ffn, w_in, w_out, hy_conv_w,
           hy_conv_b, hy_w1, hy_b1, hy_w2, hy_b2, hy_w3, hy_freq, hy_decay, hy_bias, q_norm, k_norm, lam_q1, lam_k1,
           lam_q2, lam_k2, subln, w_router, b_router, w_gu, b_gu, w_down, b_down):
    b, s, d = x.shape
    n_ctx = ctx.shape[1]
    depth = w_ada.shape[0]
    n_exp = w_router.shape[-1]

    rows = -(-(b + 1) // SUBLANES) * SUBLANES
    cc = jnp.zeros((rows, d), F32).at[:b].set(c).at[b].set(c_ctx)
    mod = _ada(cc, w_ada, b_ada)

    head_cols = jnp.concatenate([jnp.arange(HEAD_DIM) + HEAD_DIM * h for h in _GQA_HEAD_ORDER])
    in_perm = jnp.concatenate([jnp.arange(Q_START), Q_START + head_cols, jnp.arange(Q_START + GQA_Q_COLS, IN_COLS)])
    out_perm = jnp.concatenate([jnp.arange(HY_WIDTH), HY_WIDTH + head_cols, jnp.arange(HY_WIDTH + GQA_Q_COLS, d)])
    w_in_b = w_in[:, :, in_perm].astype(BF16)
    w_out_b = w_out[:, out_perm, :].astype(BF16)
    wr_p = jnp.pad(w_router, ((0, 0), (0, 0), (0, LANES - n_exp)))
    wr_hi = wr_p.astype(BF16)
    wr_p = jnp.stack([wr_hi, (wr_p - wr_hi.astype(F32)).astype(BF16)], axis=1)
    br_p = jnp.pad(b_router, ((0, 0), (0, LANES - n_exp)), constant_values=NEG_BIG)[:, None, :]

    lane = jnp.arange(LANES)
    block_diag = (lane[:, None] // HEAD_DIM == lane[None, :] // HEAD_DIM).astype(F32)
    reps = LANES // HEAD_DIM

    tabs_lat = _rope_tables(s, HEAD_DIM, True) + _rope_tables(s, DIFF_QK_DIM, True)
    tabs_ctx = _rope_tables(n_ctx, HEAD_DIM, False) + _rope_tables(n_ctx, DIFF_QK_DIM, False)
    mats_lat = _dft_mats(s)
    feats_lat = _hyena_feats(s)
    if depth > 1:
        mats_ctx = _dft_mats(n_ctx)
        feats_ctx = _hyena_feats(n_ctx)

    tm_lat = _tile_rows(s, 512)
    tm_ctx = _tile_rows(n_ctx, 256)
    lk = n_ctx + s
    kb = 256 if (tm_lat % 256 == 0 and tm_ctx % 256 == 0) else 128

    def filters(i, feats, mats):
        w1p = jnp.pad(hy_w1[i], ((0, LANES - hy_w1.shape[1]), (0, 0)))
        k = _hyena_filters(feats, w1p, hy_b1[i][None], hy_w2[i], hy_b2[i][None], hy_w3[i], hy_freq[i][None],
                           hy_decay[i][None])
        half = mats[0].shape[0] // 2
        return _dft_fwd(mats[0], k, _tile_rows(half, 512))

    xc = ctx
    for i in range(depth):
        ctx_out = i < depth - 1
        lam_init = 0.8 - 0.6 * math.exp(-0.3 * i)
        m_lat = mod[i, :b].reshape(b, 1, 6 * d)
        sh1, sc1, g1, sh2, sc2, g2 = [m_lat[:, :, j * d:(j + 1) * d] for j in range(6)]
        m_ctx = jnp.broadcast_to(mod[i, b].reshape(1, 1, 6 * d), (b, 1, 6 * d))
        csh1, csc1, cg1, csh2, csc2, cg2 = [m_ctx[:, :, j * d:(j + 1) * d] for j in range(6)]

        qg = jnp.tile(q_norm[i], reps)[None] * (HEAD_DIM ** -0.5 * LOG2E)
        kg = jnp.tile(k_norm[i], reps)[None]
        lam_p = jnp.stack([lam_q1[i], lam_k1[i], lam_q2[i], lam_k2[i]])
        proj = functools.partial(_inproj, g=g_pre_mix[i][None], w=w_in_b[i], qg=qg, kg=kg, bd=block_diag, kb=kb)

        hy_c, q_c, k_c, v_c = proj(xc, csc1, csh1, tables=tabs_ctx, tm=tm_ctx)
        hy_l, q_l, k_l, v_l = proj(x, sc1, sh1, tables=tabs_lat, tm=tm_lat)
        k_all = jnp.concatenate([k_c, k_l], axis=1)
        v_all = jnp.concatenate([v_c, v_l], axis=1)

        attn = functools.partial(_attention, lam_p=lam_p, sub=subln[i], lam_init=lam_init)
        go_l, do_l = attn(q_l, k_all, v_all, tq=_tile_rows(s, 256))
        kf_lat = filters(i, feats_lat, mats_lat)
        half_lat = mats_lat[0].shape[0] // 2
        hyo_l = _hyena(hy_l, hy_conv_w[i], hy_conv_b[i][None], hy_bias[i], kf_lat, mats_lat,
                       tm_lat, _tile_rows(half_lat, 512), tm_lat)

        post = functools.partial(_outproj, w_out=w_out_b[i], gpost=g_post_mix[i][None], gpre=g_pre_ffn[i][None],
                                 wr=wr_p[i], br=br_p[i])
        moe = functools.partial(_moe, wgu=w_gu, bgu=b_gu, wd=w_down, bd=b_down, layer=i,
                                gpost=g_post_ffn[i][None])

        xm, h2, topi, topw, cnt = post(hyo_l, go_l, do_l, x=x, g1=g1, sc2=sc2, sh2=sh2, tm=tm_lat)
        x = moe(h2, topi, topw, cnt, xm=xm, g2=g2, tm=tm_lat)

        if ctx_out:
            go_c, do_c = attn(q_c, k_c, v_c, tq=tm_ctx)
            kf_ctx = filters(i, feats_ctx, mats_ctx)
            half_ctx = mats_ctx[0].shape[0] // 2
            hyo_c = _hyena(hy_c, hy_conv_w[i], hy_conv_b[i][None], hy_bias[i], kf_ctx, mats_ctx,
                           tm_ctx, _tile_rows(half_ctx, 512), tm_ctx)
            xcm, h2c, topic, topwc, cntc = post(hyo_c, go_c, do_c, x=xc, g1=cg1, sc2=csc2, sh2=csh2, tm=tm_ctx)
            xc = moe(h2c, topic, topwc, cntc, xm=xcm, g2=cg2, tm=tm_ctx)
    return x
```
